```python
import math
import jax, jax.numpy as jnp
from jax import lax
import numpy as np

D_MODEL = 1024
BATCH = 4
SEQ = 4096
DEPTH = 2

N_MIXERS = 2
EXPAND = 2
BRANCH_WIDTH = EXPAND * D_MODEL
MEM_HEADS = 4
MEM_HEAD_DIM = 128
MEM_WIDTH = MEM_HEADS * MEM_HEAD_DIM
MIX_WIDTH = BRANCH_WIDTH - MEM_WIDTH
MEM_TOKENS = 256
DA_QK_DIM = 64
DA_V_DIM = 2 * DA_QK_DIM
DA_HEADS = MIX_WIDTH // DA_V_DIM
Q_BLOCK = 128
ROPE_THETA = 10000.0
POOL_WINDOWS = (2, 4, 8, 16)
N_POOL_GROUPS = len(POOL_WINDOWS)
POOL_GROUP_WIDTH = MIX_WIDTH // N_POOL_GROUPS
N_ATTN_LAYERS = (DEPTH + 1) // 2
N_POOL_LAYERS = DEPTH // 2
ATTN_IN_WIDTH = 3 * MIX_WIDTH + MEM_WIDTH + BRANCH_WIDTH
POOL_IN_WIDTH = MIX_WIDTH + MEM_WIDTH + BRANCH_WIDTH
POS_OFFSET_MAX = 1024
EPS = 1e-6

kernel_name = 'hybrid_diffattn_multiscale_pool_memcross'


def rms_norm(x, g):
    xf = x.astype(jnp.float32)
    y = xf * lax.rsqrt(jnp.mean(xf * xf, axis=-1, keepdims=True) + EPS)
    return (y * g.astype(jnp.float32)).astype(x.dtype)


def rope_tables(positions, dim):
    inv = ROPE_THETA ** (-jnp.arange(0, dim, 2, dtype=jnp.float32) / dim)
    ang = positions.astype(jnp.float32)[..., None] * inv
    ang = jnp.concatenate([ang, ang], axis=-1)
    return jnp.cos(ang), jnp.sin(ang)


def apply_rotary(x, cos, sin):
    x1, x2 = jnp.split(x, 2, axis=-1)
    rot = jnp.concatenate([-x2, x1], axis=-1)
    return (x.astype(jnp.float32) * cos + rot.astype(jnp.float32) * sin).astype(x.dtype)


def lambda_init_fn(layer_idx):
    return 0.8 - 0.6 * math.exp(-0.3 * layer_idx)


def diff_attention(u, positions, lam, subln_g, lambda_init):
    B, S, _ = u.shape
    q, k, v = jnp.split(u, 3, axis=-1)
    q = q.reshape(B, S, DA_HEADS, 2, DA_QK_DIM)
    k = k.reshape(B, S, DA_HEADS, 2, DA_QK_DIM)
    v = v.reshape(B, S, DA_HEADS, DA_V_DIM)
    cos, sin = rope_tables(positions, DA_QK_DIM)
    cos = cos[:, :, None, None, :]
    sin = sin[:, :, None, None, :]
    q = apply_rotary(q, cos, sin)
    k = apply_rotary(k, cos, sin)
    lf = lam.astype(jnp.float32)
    lam_full = jnp.exp(jnp.sum(lf[0] * lf[1])) - jnp.exp(jnp.sum(lf[2] * lf[3])) + lambda_init
    nb = S // Q_BLOCK
    qb = q.reshape(B, nb, Q_BLOCK, DA_HEADS, 2, DA_QK_DIM).transpose(1, 0, 2, 3, 4, 5)
    kpos = jnp.arange(S)
    scale = DA_QK_DIM ** -0.5

    def block(args):
        q_blk, start = args
        s = jnp.einsum('bqhmd,bkhmd->bhmqk', q_blk, k).astype(jnp.float32) * scale
        qpos = start + jnp.arange(Q_BLOCK)
        mask = kpos[None, :] <= qpos[:, None]
        s = jnp.where(mask, s, -jnp.inf)
        p = jax.nn.softmax(s, axis=-1)
        w = p[:, :, 0] - lam_full * p[:, :, 1]
        return jnp.einsum('bhqk,bkhe->bqhe', w.astype(v.dtype), v)

    starts = jnp.arange(nb) * Q_BLOCK
    o = lax.map(block, (qb, starts))
    o = o.transpose(1, 0, 2, 3, 4).reshape(B, S, DA_HEADS, DA_V_DIM)
    o = rms_norm(o, subln_g) * (1.0 - lambda_init)
    return o.reshape(B, S, MIX_WIDTH)


def multiscale_pool(u, w_group, scale):
    B, S, _ = u.shape
    uf = u.astype(jnp.float32)
    groups = jnp.split(uf, N_POOL_GROUPS, axis=-1)
    t_count = jnp.arange(1, S + 1)
    outs = []
    for ug, win in zip(groups, POOL_WINDOWS):
        c = jnp.cumsum(ug, axis=1)
        lag = jnp.concatenate([jnp.zeros((B, win, ug.shape[-1]), jnp.float32), c[:, :S - win]], axis=1)
        cnt = jnp.minimum(t_count, win).astype(jnp.float32)[None, :, None]
        outs.append((c - lag) / cnt - ug)
    pooled = jnp.stack(outs, axis=2).astype(u.dtype)
    mixed = jnp.einsum('bsgc,gcd->bsgd', pooled, w_group)
    return mixed.reshape(B, S, MIX_WIDTH) * scale


def memory_attention(q_mem, mem_n, w_kv):
    B, S, _ = q_mem.shape
    kv = jnp.einsum('bmd,de->bme', mem_n, w_kv)
    k, v = jnp.split(kv, 2, axis=-1)
    q = q_mem.reshape(B, S, MEM_HEADS, MEM_HEAD_DIM)
    k = k.reshape(B, MEM_TOKENS, MEM_HEADS, MEM_HEAD_DIM)
    v = v.reshape(B, MEM_TOKENS, MEM_HEADS, MEM_HEAD_DIM)
    s = jnp.einsum('bshd,bmhd->bhsm', q, k).astype(jnp.float32) * (MEM_HEAD_DIM ** -0.5)
    p = jax.nn.softmax(s, axis=-1).astype(v.dtype)
    o = jnp.einsum('bhsm,bmhd->bshd', p, v)
    return o.reshape(B, S, MEM_WIDTH)


def setup_inputs(seed: int = 0) -> dict:
    key = jax.random.key(seed)
    ks = jax.random.split(key, 16)
    f32 = jnp.float32
    x = jax.random.normal(ks[0], (BATCH, SEQ, D_MODEL), f32)
    mem = jax.random.normal(ks[1], (BATCH, MEM_TOKENS, D_MODEL), f32)
    offsets = jax.random.randint(ks[2], (BATCH, 1), 0, POS_OFFSET_MAX, dtype=jnp.int32)
    positions = (offsets + jnp.arange(SEQ, dtype=jnp.int32)[None, :]).astype(jnp.int32)
    ln_g = 1.0 + 0.02 * jax.random.normal(ks[3], (DEPTH, D_MODEL), f32)
    attn_w_in = jax.random.normal(ks[4], (N_ATTN_LAYERS, D_MODEL, ATTN_IN_WIDTH), f32) * D_MODEL ** -0.5
    attn_lambda = 0.1 * jax.random.normal(ks[5], (N_ATTN_LAYERS, 4, DA_QK_DIM), f32)
    attn_subln_g = 1.0 + 0.02 * jax.random.normal(ks[6], (N_ATTN_LAYERS, DA_V_DIM), f32)
    pool_w_in = jax.random.normal(ks[7], (N_POOL_LAYERS, D_MODEL, POOL_IN_WIDTH), f32) * D_MODEL ** -0.5
    pool_w_group = jax.random.normal(ks[8], (N_POOL_LAYERS, N_POOL_GROUPS, POOL_GROUP_WIDTH, POOL_GROUP_WIDTH), f32) * POOL_GROUP_WIDTH ** -0.5
    pool_scale = 1.0 + 0.02 * jax.random.normal(ks[9], (N_POOL_LAYERS, MIX_WIDTH), f32)
    mem_norm_g = 1.0 + 0.02 * jax.random.normal(ks[10], (D_MODEL,), f32)
    mem_w_kv = jax.random.normal(ks[11], (DEPTH, D_MODEL, 2 * MEM_WIDTH), f32) * D_MODEL ** -0.5
    w_out = jax.random.normal(ks[12], (DEPTH, BRANCH_WIDTH, D_MODEL), f32) * BRANCH_WIDTH ** -0.5
    final_g = 1.0 + 0.02 * jax.random.normal(ks[13], (D_MODEL,), f32)
    return {'x': x, 'mem': mem, 'positions': positions, 'ln_g': ln_g,
            'attn_w_in': attn_w_in, 'attn_lambda': attn_lambda, 'attn_subln_g': attn_subln_g,
            'pool_w_in': pool_w_in, 'pool_w_group': pool_w_group, 'pool_scale': pool_scale,
            'mem_norm_g': mem_norm_g, 'mem_w_kv': mem_w_kv, 'w_out': w_out, 'final_g': final_g}


def reference(x, mem, positions, ln_g, attn_w_in, attn_lambda, attn_subln_g,
              pool_w_in, pool_w_group, pool_scale, mem_norm_g, mem_w_kv, w_out, final_g):
    h = x
    mem_n = rms_norm(mem, mem_norm_g)
    for i in range(DEPTH):
        hn = rms_norm(h, ln_g[i])
        j = i // N_MIXERS
        if i % N_MIXERS == 0:
            proj = jnp.einsum('bsd,de->bse', hn, attn_w_in[j])
            u = proj[..., :3 * MIX_WIDTH]
            q_mem = proj[..., 3 * MIX_WIDTH:3 * MIX_WIDTH + MEM_WIDTH]
            gate = proj[..., 3 * MIX_WIDTH + MEM_WIDTH:]
            y = diff_attention(u, positions, attn_lambda[j], attn_subln_g[j], lambda_init_fn(i))
        else:
            proj = jnp.einsum('bsd,de->bse', hn, pool_w_in[j])
            u = proj[..., :MIX_WIDTH]
            q_mem = proj[..., MIX_WIDTH:MIX_WIDTH + MEM_WIDTH]
            gate = proj[..., MIX_WIDTH + MEM_WIDTH:]
            y = multiscale_pool(u, pool_w_group[j], pool_scale[j])
        m = memory_attention(q_mem, mem_n, mem_w_kv[i])
        z = jnp.concatenate([y, m], axis=-1) * jax.nn.silu(gate)
        h = h + jnp.einsum('bse,ed->bsd', z, w_out[i])
    return rms_norm(h, final_g)
```

```python
import functools
import math

import jax
import jax.numpy as jnp
from jax import lax
from jax.experimental import pallas as pl
from jax.experimental.pallas import tpu as pltpu

F32 = jnp.float32
BF16 = jnp.bfloat16

D_MODEL = 1024
MEM_HEADS = 4
MEM_HEAD_DIM = 128
MEM_WIDTH = MEM_HEADS * MEM_HEAD_DIM
BRANCH_WIDTH = 2 * D_MODEL
MIX_WIDTH = BRANCH_WIDTH - MEM_WIDTH
DA_QK_DIM = 64
DA_V_DIM = 2 * DA_QK_DIM
DA_HEADS = MIX_WIDTH // DA_V_DIM
ROPE_THETA = 10000.0
POOL_WINDOWS = (2, 4, 8, 16)
POOL_GROUP_WIDTH = MIX_WIDTH // len(POOL_WINDOWS)
POOL_HALO = 16
EPS = 1e-6
ATTN_IN_WIDTH = 3 * MIX_WIDTH + MEM_WIDTH + BRANCH_WIDTH
POOL_IN_WIDTH = MIX_WIDTH + MEM_WIDTH + BRANCH_WIDTH

LANES = 128
VMEM_LIMIT_BYTES = 56 * 1024 * 1024

_GATE_OFF = 0
_Q_OFF = BRANCH_WIDTH
_K_OFF = _Q_OFF + MIX_WIDTH
_V_OFF = _K_OFF + MIX_WIDTH
_QM_OFF = _V_OFF + MIX_WIDTH


def _rms(x, g):
    ms = jnp.mean(x * x, axis=-1, keepdims=True)
    return x * lax.rsqrt(ms + EPS) * g


def _silu(x):
    return x * (1.0 / (1.0 + jnp.exp(-x)))


def _mem_kv_kernel(mem_ref, g_ref, w_ref, o_ref):
    mem_n = _rms(mem_ref[0], g_ref[...]).astype(BF16)
    o_ref[0, 0] = jnp.dot(mem_n, w_ref[0], preferred_element_type=F32).astype(BF16)


def _mem_kv(mem, mem_norm_g, w_kv_bf16):
    depth = w_kv_bf16.shape[0]
    b, m, d = mem.shape
    return pl.pallas_call(
        _mem_kv_kernel,
        grid=(depth, b),
        in_specs=[
            pl.BlockSpec((1, m, d), lambda l, i: (i, 0, 0)),
            pl.BlockSpec((1, d), lambda l, i: (0, 0)),
            pl.BlockSpec((1, d, 2 * MEM_WIDTH), lambda l, i: (l, 0, 0)),
        ],
        out_specs=pl.BlockSpec((1, 1, m, 2 * MEM_WIDTH), lambda l, i: (l, i, 0, 0)),
        out_shape=jax.ShapeDtypeStruct((depth, b, m, 2 * MEM_WIDTH), BF16),
        compiler_params=pltpu.CompilerParams(vmem_limit_bytes=VMEM_LIMIT_BYTES),
        name="mem_kv",
    )(mem, mem_norm_g.reshape(1, d), w_kv_bf16)


def _inproj0_kernel(pos_ref, inv_ref, x_ref, g_ref, w_ref, o_ref, hn_s, cos_s, sin_s, *, tn):
    j = pl.program_id(1)
    lane = lax.broadcasted_iota(jnp.int32, (1, LANES), 1)
    first_half = (lane % DA_QK_DIM) < (DA_QK_DIM // 2)

    @pl.when(j == 0)
    def _():
        hn_s[...] = _rms(x_ref[...], g_ref[...]).astype(BF16)
        ang = pos_ref[...].astype(F32) * inv_ref[...]
        cos_s[...] = jnp.cos(ang)
        sn = jnp.sin(ang)
        sin_s[...] = jnp.where(first_half, -sn, sn)

    acc = jnp.dot(hn_s[...], w_ref[...], preferred_element_type=F32)

    q_lo, k_lo, v_lo = _Q_OFF // tn, _K_OFF // tn, _V_OFF // tn
    is_rope = jnp.logical_and(j >= q_lo, j < v_lo)

    @pl.when(is_rope)
    def _():
        qscale = jnp.where(j < k_lo, DA_QK_DIM ** -0.5, 1.0).astype(F32)
        cos = cos_s[...] * qscale
        sin = sin_s[...] * qscale
        for c in range(tn // LANES):
            xc = acc[:, c * LANES:(c + 1) * LANES]
            rot = jnp.where(first_half,
                            pltpu.roll(xc, LANES - DA_QK_DIM // 2, 1),
                            pltpu.roll(xc, DA_QK_DIM // 2, 1))
            o_ref[:, c * LANES:(c + 1) * LANES] = (xc * cos + rot * sin).astype(BF16)

    @pl.when(jnp.logical_not(is_rope))
    def _():
        o_ref[...] = acc.astype(BF16)


def _inproj0(h2d, pos2d, inv_row, g_row, w_bf16, *, tm=512, tn=512):
    t, d = h2d.shape
    n = w_bf16.shape[1]
    return pl.pallas_call(
        functools.partial(_inproj0_kernel, tn=tn),
        grid=(t // tm, n // tn),
        in_specs=[
            pl.BlockSpec((tm, 1), lambda i, j: (i, 0)),
            pl.BlockSpec((1, LANES), lambda i, j: (0, 0)),
            pl.BlockSpec((tm, d), lambda i, j: (i, 0)),
            pl.BlockSpec((1, d), lambda i, j: (0, 0)),
            pl.BlockSpec((d, tn), lambda i, j: (0, j)),
        ],
        out_specs=pl.BlockSpec((tm, tn), lambda i, j: (i, j)),
        out_shape=jax.ShapeDtypeStruct((t, n), BF16),
        scratch_shapes=[
            pltpu.VMEM((tm, d), BF16),
            pltpu.VMEM((tm, LANES), F32),
            pltpu.VMEM((tm, LANES), F32),
        ],
        compiler_params=pltpu.CompilerParams(
            dimension_semantics=("arbitrary", "arbitrary"),
            vmem_limit_bytes=VMEM_LIMIT_BYTES),
        name="inproj0",
    )(pos2d, inv_row, h2d, g_row, w_bf16)


def _diff_attn_kernel(lam_ref, q_ref, k_ref, v_ref, g_ref, o_ref,
                      qs_s, m_s, l_s, acc_s, *, tq, lambda_init):
    qi = pl.program_id(2)
    q = q_ref[0]
    lane = lax.broadcasted_iota(jnp.int32, (tq, LANES), 1)
    zero = jnp.zeros_like(q)
    qs_s[0:tq, :] = jnp.where(lane < DA_QK_DIM, q, zero)
    qs_s[tq:2 * tq, :] = jnp.where(lane >= DA_QK_DIM, q, zero)
    m_s[...] = jnp.full(m_s.shape, -jnp.inf, F32)
    l_s[...] = jnp.zeros(l_s.shape, F32)
    acc_s[...] = jnp.zeros(acc_s.shape, F32)

    def step(kt, masked):
        start = pl.multiple_of(kt * tq, tq)
        k = k_ref[0, pl.ds(start, tq), :]
        v = v_ref[0, pl.ds(start, tq), :]
        s = lax.dot_general(qs_s[...], k, (((1,), (1,)), ((), ())),
                            preferred_element_type=F32)
        if masked:
            row = lax.broadcasted_iota(jnp.int32, (2 * tq, tq), 0) % tq
            col = lax.broadcasted_iota(jnp.int32, (2 * tq, tq), 1)
            s = jnp.where(col <= row, s, -jnp.inf)
        m_prev = m_s[...]
        m_new = jnp.maximum(m_prev, jnp.max(s, axis=-1, keepdims=True))
        alpha = jnp.exp(m_prev - m_new)
        p = jnp.exp(s - m_new)
        l_s[...] = alpha * l_s[...] + jnp.sum(p, axis=-1, keepdims=True)
        acc_s[...] = alpha * acc_s[...] + jnp.dot(p.astype(BF16), v, preferred_element_type=F32)
        m_s[...] = m_new

    def body(kt, carry):
        step(kt, False)
        return carry

    lax.fori_loop(0, qi, body, 0)
    step(qi, True)

    lf = lam_ref[0]
    lam_full = (jnp.exp(jnp.sum(lf[0:1] * lf[1:2], axis=-1, keepdims=True))
                - jnp.exp(jnp.sum(lf[2:3] * lf[3:4], axis=-1, keepdims=True))
                + lambda_init)
    o = acc_s[0:tq, :] / l_s[0:tq, :] - lam_full * (acc_s[tq:2 * tq, :] / l_s[tq:2 * tq, :])
    o_ref[0] = (_rms(o, g_ref[...]) * (1.0 - lambda_init)).astype(BF16)


def _diff_attn(proj3d, lam, subln_g_row, lambda_init, *, tq=512):
    b, s, _ = proj3d.shape
    qb, kb, vb = _Q_OFF // LANES, _K_OFF // LANES, _V_OFF // LANES
    return pl.pallas_call(
        functools.partial(_diff_attn_kernel, tq=tq, lambda_init=lambda_init),
        grid=(b, DA_HEADS, s // tq),
        in_specs=[
            pl.BlockSpec((1, 4, DA_QK_DIM), lambda i, h, t: (0, 0, 0)),
            pl.BlockSpec((1, tq, LANES), lambda i, h, t: (i, t, qb + h)),
            pl.BlockSpec((1, s, LANES), lambda i, h, t: (i, 0, kb + h)),
            pl.BlockSpec((1, s, LANES), lambda i, h, t: (i, 0, vb + h)),
            pl.BlockSpec((1, LANES), lambda i, h, t: (0, 0)),
        ],
        out_specs=pl.BlockSpec((1, tq, LANES), lambda i, h, t: (i, t, h)),
        out_shape=jax.ShapeDtypeStruct((b, s, MIX_WIDTH), BF16),
        scratch_shapes=[
            pltpu.VMEM((2 * tq, LANES), BF16),
            pltpu.VMEM((2 * tq, 1), F32),
            pltpu.VMEM((2 * tq, 1), F32),
            pltpu.VMEM((2 * tq, LANES), F32),
        ],
        compiler_params=pltpu.CompilerParams(
            dimension_semantics=("arbitrary", "arbitrary", "arbitrary"),
            vmem_limit_bytes=VMEM_LIMIT_BYTES),
        name="diff_attn",
    )(lam, proj3d, proj3d, proj3d, subln_g_row)


def _mem_attention(qm, kv_ref, z_s, gate_m):
    for hd in range(MEM_HEADS):
        lo = hd * MEM_HEAD_DIM
        k = kv_ref[0, 0, :, lo:lo + MEM_HEAD_DIM]
        v = kv_ref[0, 0, :, MEM_WIDTH + lo:MEM_WIDTH + lo + MEM_HEAD_DIM]
        s = lax.dot_general(qm[:, lo:lo + MEM_HEAD_DIM], k, (((1,), (1,)), ((), ())),
                            preferred_element_type=F32) * (MEM_HEAD_DIM ** -0.5)
        p = jnp.exp(s - jnp.max(s, axis=-1, keepdims=True))
        l = jnp.sum(p, axis=-1, keepdims=True)
        m = jnp.dot(p.astype(BF16), v, preferred_element_type=F32) / l
        z_s[:, MIX_WIDTH + lo:MIX_WIDTH + lo + MEM_HEAD_DIM] = (
            m * _silu(gate_m[:, lo:lo + MEM_HEAD_DIM])).astype(BF16)


def _tail0_kernel(y_ref, gate_ref, qm_ref, h_ref, kv_ref, wo_ref, o_ref, z_s):
    gate = gate_ref[...].astype(F32)
    z_s[:, 0:MIX_WIDTH] = (y_ref[...].astype(F32) * _silu(gate[:, 0:MIX_WIDTH])).astype(BF16)
    _mem_attention(qm_ref[...], kv_ref, z_s, gate[:, MIX_WIDTH:])
    o_ref[...] = h_ref[...] + jnp.dot(z_s[...], wo_ref[...], preferred_element_type=F32)


def _tail0(y2d, proj2d, h2d, kv0, wo_bf16, *, seq, tm=512):
    t, d = h2d.shape
    nt = seq // tm
    return pl.pallas_call(
        _tail0_kernel,
        grid=(t // tm,),
        in_specs=[
            pl.BlockSpec((tm, MIX_WIDTH), lambda i: (i, 0)),
            pl.BlockSpec((tm, BRANCH_WIDTH), lambda i: (i, _GATE_OFF // BRANCH_WIDTH)),
            pl.BlockSpec((tm, MEM_WIDTH), lambda i: (i, _QM_OFF // MEM_WIDTH)),
            pl.BlockSpec((tm, d), lambda i: (i, 0)),
            pl.BlockSpec((1, 1) + kv0.shape[2:], lambda i: (0, i // nt, 0, 0)),
            pl.BlockSpec((BRANCH_WIDTH, d), lambda i: (0, 0)),
        ],
        out_specs=pl.BlockSpec((tm, d), lambda i: (i, 0)),
        out_shape=jax.ShapeDtypeStruct((t, d), F32),
        scratch_shapes=[pltpu.VMEM((tm, BRANCH_WIDTH), BF16)],
        compiler_params=pltpu.CompilerParams(
            dimension_semantics=("arbitrary",),
            vmem_limit_bytes=VMEM_LIMIT_BYTES),
        name="tail0",
    )(y2d, proj2d, proj2d, h2d, kv0, wo_bf16)


def _layer1_kernel(h_ref, g_ref, win_ref, wg_ref, ps_ref, kv_ref, wo_ref, fg_ref, o_ref,
                   u_s, z_s, *, tm):
    t_idx = pl.program_id(1)
    h = h_ref[...]
    hn = _rms(h, g_ref[...]).astype(BF16)

    @pl.when(t_idx == 0)
    def _():
        u_s[0:POOL_HALO, :] = jnp.zeros((POOL_HALO, MIX_WIDTH), F32)

    @pl.when(t_idx > 0)
    def _():
        u_s[0:POOL_HALO, :] = u_s[tm:tm + POOL_HALO, :]

    u_s[POOL_HALO:POOL_HALO + tm, :] = jnp.dot(hn, win_ref[:, 0:MIX_WIDTH],
                                               preferred_element_type=F32)
    qm = jnp.dot(hn, win_ref[:, MIX_WIDTH:MIX_WIDTH + MEM_WIDTH],
                 preferred_element_type=F32).astype(BF16)
    gate = jnp.dot(hn, win_ref[:, MIX_WIDTH + MEM_WIDTH:], preferred_element_type=F32)

    pos = t_idx * tm + lax.broadcasted_iota(jnp.int32, (tm, 1), 0)
    for gi, win in enumerate(POOL_WINDOWS):
        lo = gi * POOL_GROUP_WIDTH
        hi = lo + POOL_GROUP_WIDTH
        ug = u_s[POOL_HALO:POOL_HALO + tm, lo:hi]
        wsum = ug
        for back in range(1, win):
            wsum = wsum + u_s[POOL_HALO - back:POOL_HALO - back + tm, lo:hi]
        inv_cnt = 1.0 / jnp.minimum(pos + 1, win).astype(F32)
        pooled = (wsum * inv_cnt - ug).astype(BF16)
        mixed = jnp.dot(pooled, wg_ref[gi], preferred_element_type=F32) * ps_ref[:, lo:hi]
        z_s[:, lo:hi] = (mixed * _silu(gate[:, lo:hi])).astype(BF16)

    _mem_attention(qm, kv_ref, z_s, gate[:, MIX_WIDTH:])
    h2 = h + jnp.dot(z_s[...], wo_ref[...], preferred_element_type=F32)
    o_ref[...] = _rms(h2, fg_ref[...])


def _layer1(h2d, g_row, win_bf16, wg_bf16, ps_row, kv1, wo_bf16, fg_row, *, seq, tm=256):
    t, d = h2d.shape
    b = t // seq
    nt = seq // tm
    const2 = lambda i, j: (0, 0)
    return pl.pallas_call(
        functools.partial(_layer1_kernel, tm=tm),
        grid=(b, nt),
        in_specs=[
            pl.BlockSpec((tm, d), lambda i, j: (i * nt + j, 0)),
            pl.BlockSpec((1, d), const2),
            pl.BlockSpec((d, POOL_IN_WIDTH), const2),
            pl.BlockSpec(wg_bf16.shape, lambda i, j: (0, 0, 0)),
            pl.BlockSpec((1, MIX_WIDTH), const2),
            pl.BlockSpec((1, 1) + kv1.shape[2:], lambda i, j: (1, i, 0, 0)),
            pl.BlockSpec((BRANCH_WIDTH, d), const2),
            pl.BlockSpec((1, d), const2),
        ],
        out_specs=pl.BlockSpec((tm, d), lambda i, j: (i * nt + j, 0)),
        out_shape=jax.ShapeDtypeStruct((t, d), F32),
        scratch_shapes=[
            pltpu.VMEM((POOL_HALO + tm, MIX_WIDTH), F32),
            pltpu.VMEM((tm, BRANCH_WIDTH), BF16),
        ],
        compiler_params=pltpu.CompilerParams(
            dimension_semantics=("arbitrary", "arbitrary"),
            vmem_limit_bytes=VMEM_LIMIT_BYTES),
        name="layer1",
    )(h2d, g_row, win_bf16, wg_bf16, ps_row, kv1, wo_bf16, fg_row)


def kernel(x, mem, positions, ln_g, attn_w_in, attn_lambda, attn_subln_g, pool_w_in,
           pool_w_group, pool_scale, mem_norm_g, mem_w_kv, w_out, final_g):
    b, s, d = x.shape
    t = b * s
    lambda_init0 = 0.8 - 0.6 * math.exp(-0.3 * 0)

    w0 = attn_w_in[0]
    w0 = jnp.concatenate([w0[:, 3 * MIX_WIDTH + MEM_WIDTH:], w0[:, :3 * MIX_WIDTH + MEM_WIDTH]],
                         axis=1).astype(BF16)
    w1 = pool_w_in[0].astype(BF16)
    wg = pool_w_group[0].astype(BF16)
    wkv = mem_w_kv.astype(BF16)
    wo = w_out.astype(BF16)
    inv = ROPE_THETA ** (-jnp.arange(0, DA_QK_DIM, 2, dtype=F32) / DA_QK_DIM)
    inv_row = jnp.tile(inv, LANES // inv.shape[0]).reshape(1, LANES)

    h2d = x.reshape(t, d)
    kv = _mem_kv(mem, mem_norm_g, wkv)

    proj = _inproj0(h2d, positions.reshape(t, 1), inv_row, ln_g[0].reshape(1, d), w0)
    y = _diff_attn(proj.reshape(b, s, ATTN_IN_WIDTH), attn_lambda[0:1],
                   attn_subln_g[0].reshape(1, DA_V_DIM), lambda_init0)
    h1 = _tail0(y.reshape(t, MIX_WIDTH), proj, h2d, kv, wo[0], seq=s)
    out = _layer1(h1, ln_g[1].reshape(1, d), w1, wg, pool_scale[0].reshape(1, MIX_WIDTH),
                  kv, wo[1], final_g.reshape(1, d), seq=s)
    return out.reshape(b, s, d)
```

```python
import functools
import math

import jax
import jax.numpy as jnp
from jax import lax
from jax.experimental import pallas as pl
from jax.experimental.pallas import tpu as pltpu

F32 = jnp.float32
BF16 = jnp.bfloat16

D_MODEL = 1024
MEM_HEADS = 4
MEM_HEAD_DIM = 128
MEM_WIDTH = MEM_HEADS * MEM_HEAD_DIM
BRANCH_WIDTH = 2 * D_MODEL
MIX_WIDTH = BRANCH_WIDTH - MEM_WIDTH
DA_QK_DIM = 64
DA_V_DIM = 2 * DA_QK_DIM
DA_HEADS = MIX_WIDTH // DA_V_DIM
ROPE_THETA = 10000.0
POOL_WINDOWS = (2, 4, 8, 16)
POOL_GROUP_WIDTH = MIX_WIDTH // len(POOL_WINDOWS)
POOL_HALO = 16
EPS = 1e-6
ATTN_IN_WIDTH = 3 * MIX_WIDTH + MEM_WIDTH + BRANCH_WIDTH
POOL_IN_WIDTH = MIX_WIDTH + MEM_WIDTH + BRANCH_WIDTH

LANES = 128
VMEM_LIMIT_BYTES = 56 * 1024 * 1024

_GATE_OFF = 0
_Q_OFF = BRANCH_WIDTH
_K_OFF = _Q_OFF + MIX_WIDTH
_V_OFF = _K_OFF + MIX_WIDTH
_QM_OFF = _V_OFF + MIX_WIDTH


def _rms(x, g):
    ms = jnp.mean(x * x, axis=-1, keepdims=True)
    return x * lax.rsqrt(ms + EPS) * g


def _silu(x):
    return x * (1.0 / (1.0 + jnp.exp(-x)))


def _mem_kv_kernel(mem_ref, g_ref, w_ref, o_ref):
    mem_n = _rms(mem_ref[0], g_ref[...]).astype(BF16)
    o_ref[0, 0] = jnp.dot(mem_n, w_ref[0], preferred_element_type=F32).astype(BF16)


def _mem_kv(mem, mem_norm_g, w_kv_bf16):
    depth = w_kv_bf16.shape[0]
    b, m, d = mem.shape
    return pl.pallas_call(
        _mem_kv_kernel,
        grid=(depth, b),
        in_specs=[
            pl.BlockSpec((1, m, d), lambda l, i: (i, 0, 0)),
            pl.BlockSpec((1, d), lambda l, i: (0, 0)),
            pl.BlockSpec((1, d, 2 * MEM_WIDTH), lambda l, i: (l, 0, 0)),
        ],
        out_specs=pl.BlockSpec((1, 1, m, 2 * MEM_WIDTH), lambda l, i: (l, i, 0, 0)),
        out_shape=jax.ShapeDtypeStruct((depth, b, m, 2 * MEM_WIDTH), BF16),
        compiler_params=pltpu.CompilerParams(vmem_limit_bytes=VMEM_LIMIT_BYTES),
        name="mem_kv",
    )(mem, mem_norm_g.reshape(1, d), w_kv_bf16)


def _inproj0_kernel(pos_ref, inv_ref, x_ref, g_ref, w_ref, o_ref, hn_s, cos_s, sin_s, *, tn):
    j = pl.program_id(1)
    lane = lax.broadcasted_iota(jnp.int32, (1, LANES), 1)
    first_half = (lane % DA_QK_DIM) < (DA_QK_DIM // 2)

    @pl.when(j == 0)
    def _():
        hn_s[...] = _rms(x_ref[...], g_ref[...]).astype(BF16)
        ang = pos_ref[...].astype(F32) * inv_ref[...]
        cos_s[...] = jnp.cos(ang)
        sn = jnp.sin(ang)
        sin_s[...] = jnp.where(first_half, -sn, sn)

    acc = jnp.dot(hn_s[...], w_ref[...], preferred_element_type=F32)

    q_lo, k_lo, v_lo = _Q_OFF // tn, _K_OFF // tn, _V_OFF // tn
    is_rope = jnp.logical_and(j >= q_lo, j < v_lo)

    @pl.when(is_rope)
    def _():
        qscale = jnp.where(j < k_lo, DA_QK_DIM ** -0.5 * math.log2(math.e), 1.0).astype(F32)
        cos = cos_s[...] * qscale
        sin = sin_s[...] * qscale
        for c in range(tn // LANES):
            xc = acc[:, c * LANES:(c + 1) * LANES]
            rot = jnp.where(first_half,
                            pltpu.roll(xc, LANES - DA_QK_DIM // 2, 1),
                            pltpu.roll(xc, DA_QK_DIM // 2, 1))
            o_ref[:, c * LANES:(c + 1) * LANES] = (xc * cos + rot * sin).astype(BF16)

    @pl.when(jnp.logical_not(is_rope))
    def _():
        o_ref[...] = acc.astype(BF16)


def _inproj0(h2d, pos2d, inv_row, g_row, w_bf16, *, tm=512, tn=512):
    t, d = h2d.shape
    n = w_bf16.shape[1]
    return pl.pallas_call(
        functools.partial(_inproj0_kernel, tn=tn),
        grid=(t // tm, n // tn),
        in_specs=[
            pl.BlockSpec((tm, 1), lambda i, j: (i, 0)),
            pl.BlockSpec((1, LANES), lambda i, j: (0, 0)),
            pl.BlockSpec((tm, d), lambda i, j: (i, 0)),
            pl.BlockSpec((1, d), lambda i, j: (0, 0)),
            pl.BlockSpec((d, tn), lambda i, j: (0, j)),
        ],
        out_specs=pl.BlockSpec((tm, tn), lambda i, j: (i, j)),
        out_shape=jax.ShapeDtypeStruct((t, n), BF16),
        scratch_shapes=[
            pltpu.VMEM((tm, d), BF16),
            pltpu.VMEM((tm, LANES), F32),
            pltpu.VMEM((tm, LANES), F32),
        ],
        compiler_params=pltpu.CompilerParams(
            dimension_semantics=("arbitrary", "arbitrary"),
            vmem_limit_bytes=VMEM_LIMIT_BYTES),
        name="inproj0",
    )(pos2d, inv_row, h2d, g_row, w_bf16)


def _diff_attn_kernel(lam_ref, q_ref, k_ref, v_ref, g_ref, o_ref,
                      qs_s, vx_s, m_s, acc_s, *, tq, rc, lambda_init):
    qi = pl.program_id(2)

    @pl.when(qi == 0)
    def _():
        vx_s[:, 0:LANES] = v_ref[0]
        vx_s[:, LANES:2 * LANES] = jnp.ones((vx_s.shape[0], LANES), BF16)

    q = q_ref[0]
    lane = lax.broadcasted_iota(jnp.int32, (tq, LANES), 1)
    zero = jnp.zeros_like(q)
    qs_s[0:tq, :] = jnp.where(lane < DA_QK_DIM, q, zero)
    qs_s[tq:2 * tq, :] = jnp.where(lane >= DA_QK_DIM, q, zero)
    m_s[...] = jnp.full(m_s.shape, -jnp.inf, F32)
    acc_s[...] = jnp.zeros(acc_s.shape, F32)

    def chunk(r, start, ncols, masked):
        k = k_ref[0, pl.ds(start, ncols), :]
        vx = vx_s[pl.ds(start, ncols), :]
        s = lax.dot_general(qs_s[r:r + rc, :], k, (((1,), (1,)), ((), ())),
                            preferred_element_type=F32)
        cols = [s[:, c:c + LANES] for c in range(0, ncols, LANES)]
        if masked:
            row = lax.broadcasted_iota(jnp.int32, (rc, LANES), 0)
            col = lax.broadcasted_iota(jnp.int32, (rc, LANES), 1)
            for i in range(rc // LANES):
                ci = (ncols - rc) // LANES + i
                cols[ci] = jnp.where(col + i * LANES <= row, cols[ci], -jnp.inf)
        m_prev = m_s[r:r + rc, :]
        m_cur = cols[0]
        for c in cols[1:]:
            m_cur = jnp.maximum(m_cur, c)
        m_new = jnp.maximum(m_prev, jnp.max(m_cur, axis=-1, keepdims=True))
        alpha = jnp.exp2(m_prev - m_new)
        p = jnp.concatenate([jnp.exp2(c - m_new) for c in cols], axis=1).astype(BF16)
        pv = jnp.dot(p, vx, preferred_element_type=F32)
        acc_s[r:r + rc, :] = jnp.concatenate([alpha, alpha], axis=1) * acc_s[r:r + rc, :] + pv
        m_s[r:r + rc, :] = m_new

    def body(kt, carry):
        start = pl.multiple_of(kt * tq, tq)
        for r in range(0, 2 * tq, rc):
            chunk(r, start, tq, False)
        return carry

    lax.fori_loop(0, qi, body, 0)
    diag = pl.multiple_of(qi * tq, tq)
    for r in range(0, 2 * tq, rc):
        chunk(r, diag, r % tq + rc, True)

    lf = lam_ref[0]
    lam_full = (jnp.exp(jnp.sum(lf[0:1] * lf[1:2], axis=-1, keepdims=True))
                - jnp.exp(jnp.sum(lf[2:3] * lf[3:4], axis=-1, keepdims=True))
                + lambda_init)
    o = (acc_s[0:tq, 0:LANES] / acc_s[0:tq, LANES:2 * LANES]
         - lam_full * (acc_s[tq:2 * tq, 0:LANES] / acc_s[tq:2 * tq, LANES:2 * LANES]))
    o_ref[0] = (_rms(o, g_ref[...]) * (1.0 - lambda_init)).astype(BF16)


def _diff_attn(proj3d, lam, subln_g_row, lambda_init, *, tq=512, rc=256):
    b, s, _ = proj3d.shape
    qb, kb, vb = _Q_OFF // LANES, _K_OFF // LANES, _V_OFF // LANES
    return pl.pallas_call(
        functools.partial(_diff_attn_kernel, tq=tq, rc=rc, lambda_init=lambda_init),
        grid=(b, DA_HEADS, s // tq),
        in_specs=[
            pl.BlockSpec((1, 4, DA_QK_DIM), lambda i, h, t: (0, 0, 0)),
            pl.BlockSpec((1, tq, LANES), lambda i, h, t: (i, t, qb + h)),
            pl.BlockSpec((1, s, LANES), lambda i, h, t: (i, 0, kb + h)),
            pl.BlockSpec((1, s, LANES), lambda i, h, t: (i, 0, vb + h)),
            pl.BlockSpec((1, LANES), lambda i, h, t: (0, 0)),
        ],
        out_specs=pl.BlockSpec((1, tq, LANES), lambda i, h, t: (i, t, h)),
        out_shape=jax.ShapeDtypeStruct((b, s, MIX_WIDTH), BF16),
        scratch_shapes=[
            pltpu.VMEM((2 * tq, LANES), BF16),
            pltpu.VMEM((s, 2 * LANES), BF16),
            pltpu.VMEM((2 * tq, LANES), F32),
            pltpu.VMEM((2 * tq, 2 * LANES), F32),
        ],
        compiler_params=pltpu.CompilerParams(
            dimension_semantics=("arbitrary", "arbitrary", "arbitrary"),
            vmem_limit_bytes=VMEM_LIMIT_BYTES),
        name="diff_attn",
    )(lam, proj3d, proj3d, proj3d, subln_g_row)


def _mem_attention(qm, kv_ref, z_s, gate_m):
    for hd in range(MEM_HEADS):
        lo = hd * MEM_HEAD_DIM
        k = kv_ref[0, 0, :, lo:lo + MEM_HEAD_DIM]
        v = kv_ref[0, 0, :, MEM_WIDTH + lo:MEM_WIDTH + lo + MEM_HEAD_DIM]
        s = lax.dot_general(qm[:, lo:lo + MEM_HEAD_DIM], k, (((1,), (1,)), ((), ())),
                            preferred_element_type=F32) * (MEM_HEAD_DIM ** -0.5)
        p = jnp.exp(s - jnp.max(s, axis=-1, keepdims=True))
        l = jnp.sum(p, axis=-1, keepdims=True)
        m = jnp.dot(p.astype(BF16), v, preferred_element_type=F32) / l
        z_s[:, MIX_WIDTH + lo:MIX_WIDTH + lo + MEM_HEAD_DIM] = (
            m * _silu(gate_m[:, lo:lo + MEM_HEAD_DIM])).astype(BF16)


def _tail0_kernel(y_ref, gate_ref, qm_ref, h_ref, kv_ref, wo_ref, o_ref, z_s):
    gate = gate_ref[...].astype(F32)
    z_s[:, 0:MIX_WIDTH] = (y_ref[...].astype(F32) * _silu(gate[:, 0:MIX_WIDTH])).astype(BF16)
    _mem_attention(qm_ref[...], kv_ref, z_s, gate[:, MIX_WIDTH:])
    o_ref[...] = h_ref[...] + jnp.dot(z_s[...], wo_ref[...], preferred_element_type=F32)


def _tail0(y2d, proj2d, h2d, kv0, wo_bf16, *, seq, tm=512):
    t, d = h2d.shape
    nt = seq // tm
    return pl.pallas_call(
        _tail0_kernel,
        grid=(t // tm,),
        in_specs=[
            pl.BlockSpec((tm, MIX_WIDTH), lambda i: (i, 0)),
            pl.BlockSpec((tm, BRANCH_WIDTH), lambda i: (i, _GATE_OFF // BRANCH_WIDTH)),
            pl.BlockSpec((tm, MEM_WIDTH), lambda i: (i, _QM_OFF // MEM_WIDTH)),
            pl.BlockSpec((tm, d), lambda i: (i, 0)),
            pl.BlockSpec((1, 1) + kv0.shape[2:], lambda i: (0, i // nt, 0, 0)),
            pl.BlockSpec((BRANCH_WIDTH, d), lambda i: (0, 0)),
        ],
        out_specs=pl.BlockSpec((tm, d), lambda i: (i, 0)),
        out_shape=jax.ShapeDtypeStruct((t, d), F32),
        scratch_shapes=[pltpu.VMEM((tm, BRANCH_WIDTH), BF16)],
        compiler_params=pltpu.CompilerParams(
            dimension_semantics=("arbitrary",),
            vmem_limit_bytes=VMEM_LIMIT_BYTES),
        name="tail0",
    )(y2d, proj2d, proj2d, h2d, kv0, wo_bf16)


def _layer1_kernel(h_ref, g_ref, win_ref, wg_ref, ps_ref, kv_ref, wo_ref, fg_ref, o_ref,
                   u_s, z_s, *, tm):
    t_idx = pl.program_id(1)
    h = h_ref[...]
    hn = _rms(h, g_ref[...]).astype(BF16)

    @pl.when(t_idx == 0)
    def _():
        u_s[0:POOL_HALO, :] = jnp.zeros((POOL_HALO, MIX_WIDTH), F32)

    @pl.when(t_idx > 0)
    def _():
        u_s[0:POOL_HALO, :] = u_s[tm:tm + POOL_HALO, :]

    u_s[POOL_HALO:POOL_HALO + tm, :] = jnp.dot(hn, win_ref[:, 0:MIX_WIDTH],
                                               preferred_element_type=F32)
    qm = jnp.dot(hn, win_ref[:, MIX_WIDTH:MIX_WIDTH + MEM_WIDTH],
                 preferred_element_type=F32).astype(BF16)
    gate = jnp.dot(hn, win_ref[:, MIX_WIDTH + MEM_WIDTH:], preferred_element_type=F32)

    pos = t_idx * tm + lax.broadcasted_iota(jnp.int32, (tm, 1), 0)
    for gi, win in enumerate(POOL_WINDOWS):
        lo = gi * POOL_GROUP_WIDTH
        hi = lo + POOL_GROUP_WIDTH
        ug = u_s[POOL_HALO:POOL_HALO + tm, lo:hi]
        wsum = ug
        for back in range(1, win):
            wsum = wsum + u_s[POOL_HALO - back:POOL_HALO - back + tm, lo:hi]
        inv_cnt = 1.0 / jnp.minimum(pos + 1, win).astype(F32)
        pooled = (wsum * inv_cnt - ug).astype(BF16)
        mixed = jnp.dot(pooled, wg_ref[gi], preferred_element_type=F32) * ps_ref[:, lo:hi]
        z_s[:, lo:hi] = (mixed * _silu(gate[:, lo:hi])).astype(BF16)

    _mem_attention(qm, kv_ref, z_s, gate[:, MIX_WIDTH:])
    h2 = h + jnp.dot(z_s[...], wo_ref[...], preferred_element_type=F32)
    o_ref[...] = _rms(h2, fg_ref[...])


def _layer1(h2d, g_row, win_bf16, wg_bf16, ps_row, kv1, wo_bf16, fg_row, *, seq, tm=256):
    t, d = h2d.shape
    b = t // seq
    nt = seq // tm
    const2 = lambda i, j: (0, 0)
    return pl.pallas_call(
        functools.partial(_layer1_kernel, tm=tm),
        grid=(b, nt),
        in_specs=[
            pl.BlockSpec((tm, d), lambda i, j: (i * nt + j, 0)),
            pl.BlockSpec((1, d), const2),
            pl.BlockSpec((d, POOL_IN_WIDTH), const2),
            pl.BlockSpec(wg_bf16.shape, lambda i, j: (0, 0, 0)),
            pl.BlockSpec((1, MIX_WIDTH), const2),
            pl.BlockSpec((1, 1) + kv1.shape[2:], lambda i, j: (1, i, 0, 0)),
            pl.BlockSpec((BRANCH_WIDTH, d), const2),
            pl.BlockSpec((1, d), const2),
        ],
        out_specs=pl.BlockSpec((tm, d), lambda i, j: (i * nt + j, 0)),
        out_shape=jax.ShapeDtypeStruct((t, d), F32),
        scratch_shapes=[
            pltpu.VMEM((POOL_HALO + tm, MIX_WIDTH), F32),
            pltpu.VMEM((tm, BRANCH_WIDTH), BF16),
        ],
        compiler_params=pltpu.CompilerParams(
            dimension_semantics=("arbitrary", "arbitrary"),
            vmem_limit_bytes=VMEM_LIMIT_BYTES),
        name="layer1",
    )(h2d, g_row, win_bf16, wg_bf16, ps_row, kv1, wo_bf16, fg_row)


def kernel(x, mem, positions, ln_g, attn_w_in, attn_lambda, attn_subln_g, pool_w_in,
           pool_w_group, pool_scale, mem_norm_g, mem_w_kv, w_out, final_g):
    b, s, d = x.shape
    t = b * s
    lambda_init0 = 0.8 - 0.6 * math.exp(-0.3 * 0)

    w0 = attn_w_in[0]
    w0 = jnp.concatenate([w0[:, 3 * MIX_WIDTH + MEM_WIDTH:], w0[:, :3 * MIX_WIDTH + MEM_WIDTH]],
                         axis=1).astype(BF16)
    w1 = pool_w_in[0].astype(BF16)
    wg = pool_w_group[0].astype(BF16)
    wkv = mem_w_kv.astype(BF16)
    wo = w_out.astype(BF16)
    inv = ROPE_THETA ** (-jnp.arange(0, DA_QK_DIM, 2, dtype=F32) / DA_QK_DIM)
    inv_row = jnp.tile(inv, LANES // inv.shape[0]).reshape(1, LANES)

    h2d = x.reshape(t, d)
    kv = _mem_kv(mem, mem_norm_g, wkv)

    proj = _inproj0(h2d, positions.reshape(t, 1), inv_row, ln_g[0].reshape(1, d), w0)
    y = _diff_attn(proj.reshape(b, s, ATTN_IN_WIDTH), attn_lambda[0:1],
                   attn_subln_g[0].reshape(1, DA_V_DIM), lambda_init0)
    h1 = _tail0(y.reshape(t, MIX_WIDTH), proj, h2d, kv, wo[0], seq=s)
    out = _layer1(h1, ln_g[1].reshape(1, d), w1, wg, pool_scale[0].reshape(1, MIX_WIDTH),
                  kv, wo[1], final_g.reshape(1, d), seq=s)
    return out.reshape(b, s, d)
```

```python
import functools
import math

import jax
import jax.numpy as jnp
from jax import lax
from jax.experimental import pallas as pl
from jax.experimental.pallas import tpu as pltpu

F32 = jnp.float32
BF16 = jnp.bfloat16

D_MODEL = 1024
MEM_HEADS = 4
MEM_HEAD_DIM = 128
MEM_WIDTH = MEM_HEADS * MEM_HEAD_DIM
BRANCH_WIDTH = 2 * D_MODEL
MIX_WIDTH = BRANCH_WIDTH - MEM_WIDTH
DA_QK_DIM = 64
DA_V_DIM = 2 * DA_QK_DIM
DA_HEADS = MIX_WIDTH // DA_V_DIM
ROPE_THETA = 10000.0
POOL_WINDOWS = (2, 4, 8, 16)
POOL_GROUP_WIDTH = MIX_WIDTH // len(POOL_WINDOWS)
POOL_HALO = 16
EPS = 1e-6
ATTN_IN_WIDTH = 3 * MIX_WIDTH + MEM_WIDTH + BRANCH_WIDTH
POOL_IN_WIDTH = MIX_WIDTH + MEM_WIDTH + BRANCH_WIDTH

LANES = 128
VMEM_LIMIT_BYTES = 56 * 1024 * 1024

_GATE_OFF = 0
_Q_OFF = BRANCH_WIDTH
_K_OFF = _Q_OFF + MIX_WIDTH
_V_OFF = _K_OFF + MIX_WIDTH
_QM_OFF = _V_OFF + MIX_WIDTH


def _rms(x, g):
    ms = jnp.mean(x * x, axis=-1, keepdims=True)
    return x * lax.rsqrt(ms + EPS) * g


def _silu(x):
    return x * (1.0 / (1.0 + jnp.exp(-x)))


def _mem_kv_kernel(mem_ref, g_ref, w_ref, o_ref):
    mem_n = _rms(mem_ref[0], g_ref[...]).astype(BF16)
    o_ref[0, 0] = jnp.dot(mem_n, w_ref[0], preferred_element_type=F32).astype(BF16)


def _mem_kv(mem, mem_norm_g, w_kv_bf16):
    depth = w_kv_bf16.shape[0]
    b, m, d = mem.shape
    return pl.pallas_call(
        _mem_kv_kernel,
        grid=(depth, b),
        in_specs=[
            pl.BlockSpec((1, m, d), lambda l, i: (i, 0, 0)),
            pl.BlockSpec((1, d), lambda l, i: (0, 0)),
            pl.BlockSpec((1, d, 2 * MEM_WIDTH), lambda l, i: (l, 0, 0)),
        ],
        out_specs=pl.BlockSpec((1, 1, m, 2 * MEM_WIDTH), lambda l, i: (l, i, 0, 0)),
        out_shape=jax.ShapeDtypeStruct((depth, b, m, 2 * MEM_WIDTH), BF16),
        compiler_params=pltpu.CompilerParams(vmem_limit_bytes=VMEM_LIMIT_BYTES),
        name="mem_kv",
    )(mem, mem_norm_g.reshape(1, d), w_kv_bf16)


def _inproj0_kernel(pos_ref, inv_ref, x_ref, g_ref, w_ref, o_ref, hn_s, cos_s, sin_s, *, tn):
    lane = lax.broadcasted_iota(jnp.int32, (1, LANES), 1)
    first_half = (lane % DA_QK_DIM) < (DA_QK_DIM // 2)
    hn_s[...] = _rms(x_ref[...], g_ref[...]).astype(BF16)
    ang = pos_ref[...].astype(F32) * inv_ref[...]
    cos_s[...] = jnp.cos(ang)
    sn = jnp.sin(ang)
    sin_s[...] = jnp.where(first_half, -sn, sn)

    qscale = DA_QK_DIM ** -0.5 * math.log2(math.e)
    for lo in range(0, w_ref.shape[1], tn):
        acc = jnp.dot(hn_s[...], w_ref[:, lo:lo + tn], preferred_element_type=F32)
        if _Q_OFF <= lo < _V_OFF:
            scale = qscale if lo < _K_OFF else 1.0
            cos = cos_s[...] * scale
            sin = sin_s[...] * scale
            for c in range(0, tn, LANES):
                xc = acc[:, c:c + LANES]
                rot = jnp.where(first_half,
                                pltpu.roll(xc, LANES - DA_QK_DIM // 2, 1),
                                pltpu.roll(xc, DA_QK_DIM // 2, 1))
                o_ref[:, lo + c:lo + c + LANES] = (xc * cos + rot * sin).astype(BF16)
        else:
            o_ref[:, lo:lo + tn] = acc.astype(BF16)


def _inproj0(h2d, pos2d, inv_row, g_row, w_bf16, *, tm=512, tn=512):
    t, d = h2d.shape
    n = w_bf16.shape[1]
    return pl.pallas_call(
        functools.partial(_inproj0_kernel, tn=tn),
        grid=(t // tm,),
        in_specs=[
            pl.BlockSpec((tm, 1), lambda i: (i, 0)),
            pl.BlockSpec((1, LANES), lambda i: (0, 0)),
            pl.BlockSpec((tm, d), lambda i: (i, 0)),
            pl.BlockSpec((1, d), lambda i: (0, 0)),
            pl.BlockSpec((d, n), lambda i: (0, 0), pipeline_mode=pl.Buffered(1)),
        ],
        out_specs=pl.BlockSpec((tm, n), lambda i: (i, 0)),
        out_shape=jax.ShapeDtypeStruct((t, n), BF16),
        scratch_shapes=[
            pltpu.VMEM((tm, d), BF16),
            pltpu.VMEM((tm, LANES), F32),
            pltpu.VMEM((tm, LANES), F32),
        ],
        compiler_params=pltpu.CompilerParams(
            dimension_semantics=("arbitrary",),
            vmem_limit_bytes=VMEM_LIMIT_BYTES),
        name="inproj0",
    )(pos2d, inv_row, h2d, g_row, w_bf16)


def _diff_attn_kernel(lam_ref, q_ref, k_ref, v_ref, g_ref, o_ref,
                      qs_s, vx_s, s_s, m_s, acc_s, *, tq, rc, lambda_init):
    qi = pl.program_id(2)

    @pl.when(qi == 0)
    def _():
        vx_s[:, 0:LANES] = v_ref[0]
        vx_s[:, LANES:2 * LANES] = jnp.ones((vx_s.shape[0], LANES), BF16)

    q = q_ref[0]
    lane = lax.broadcasted_iota(jnp.int32, (tq, LANES), 1)
    zero = jnp.zeros_like(q)
    qs_s[0:tq, :] = jnp.where(lane < DA_QK_DIM, q, zero)
    qs_s[tq:2 * tq, :] = jnp.where(lane >= DA_QK_DIM, q, zero)
    m_s[...] = jnp.full(m_s.shape, -jnp.inf, F32)
    acc_s[...] = jnp.zeros(acc_s.shape, F32)

    def scores(r, start):
        k = k_ref[0, pl.ds(start, tq), :]
        return lax.dot_general(qs_s[r:r + rc, :], k, (((1,), (1,)), ((), ())),
                               preferred_element_type=F32)

    def update(r, s, start, ncols, masked):
        vx = vx_s[pl.ds(start, ncols), :]
        cols = [s[:, c:c + LANES] for c in range(0, ncols, LANES)]
        if masked:
            row = lax.broadcasted_iota(jnp.int32, (rc, LANES), 0)
            col = lax.broadcasted_iota(jnp.int32, (rc, LANES), 1)
            for i in range(rc // LANES):
                ci = (ncols - rc) // LANES + i
                cols[ci] = jnp.where(col + i * LANES <= row, cols[ci], -jnp.inf)
        m_prev = m_s[r:r + rc, :]
        m_cur = cols[0]
        for c in cols[1:]:
            m_cur = jnp.maximum(m_cur, c)
        m_new = jnp.maximum(m_prev, jnp.max(m_cur, axis=-1, keepdims=True))
        alpha = jnp.exp2(m_prev - m_new)
        p = jnp.concatenate([jnp.exp2(c - m_new) for c in cols], axis=1).astype(BF16)
        pv = jnp.dot(p, vx, preferred_element_type=F32)
        acc_s[r:r + rc, :] = jnp.concatenate([alpha, alpha], axis=1) * acc_s[r:r + rc, :] + pv
        m_s[r:r + rc, :] = m_new

    for r in range(0, 2 * tq, rc):
        s_s[r:r + rc, :] = scores(r, 0)

    def body(kt, carry):
        start = pl.multiple_of(kt * tq, tq)
        for r in range(0, 2 * tq, rc):
            s = s_s[r:r + rc, :]
            s_s[r:r + rc, :] = scores(r, start + tq)
            update(r, s, start, tq, False)
        return carry

    lax.fori_loop(0, qi, body, 0)
    diag = pl.multiple_of(qi * tq, tq)
    for r in range(0, 2 * tq, rc):
        ncols = r % tq + rc
        update(r, s_s[r:r + rc, 0:ncols], diag, ncols, True)

    lf = lam_ref[0]
    lam_full = (jnp.exp(jnp.sum(lf[0:1] * lf[1:2], axis=-1, keepdims=True))
                - jnp.exp(jnp.sum(lf[2:3] * lf[3:4], axis=-1, keepdims=True))
                + lambda_init)
    o = (acc_s[0:tq, 0:LANES] / acc_s[0:tq, LANES:2 * LANES]
         - lam_full * (acc_s[tq:2 * tq, 0:LANES] / acc_s[tq:2 * tq, LANES:2 * LANES]))
    o_ref[0] = (_rms(o, g_ref[...]) * (1.0 - lambda_init)).astype(BF16)


def _diff_attn(proj3d, lam, subln_g_row, lambda_init, *, tq=1024, rc=256):
    b, s, _ = proj3d.shape
    qb, kb, vb = _Q_OFF // LANES, _K_OFF // LANES, _V_OFF // LANES
    return pl.pallas_call(
        functools.partial(_diff_attn_kernel, tq=tq, rc=rc, lambda_init=lambda_init),
        grid=(b, DA_HEADS, s // tq),
        in_specs=[
            pl.BlockSpec((1, 4, DA_QK_DIM), lambda i, h, t: (0, 0, 0)),
            pl.BlockSpec((1, tq, LANES), lambda i, h, t: (i, t, qb + h)),
            pl.BlockSpec((1, s, LANES), lambda i, h, t: (i, 0, kb + h)),
            pl.BlockSpec((1, s, LANES), lambda i, h, t: (i, 0, vb + h)),
            pl.BlockSpec((1, LANES), lambda i, h, t: (0, 0)),
        ],
        out_specs=pl.BlockSpec((1, tq, LANES), lambda i, h, t: (i, t, h)),
        out_shape=jax.ShapeDtypeStruct((b, s, MIX_WIDTH), BF16),
        scratch_shapes=[
            pltpu.VMEM((2 * tq, LANES), BF16),
            pltpu.VMEM((s, 2 * LANES), BF16),
            pltpu.VMEM((2 * tq, tq), F32),
            pltpu.VMEM((2 * tq, LANES), F32),
            pltpu.VMEM((2 * tq, 2 * LANES), F32),
        ],
        compiler_params=pltpu.CompilerParams(
            dimension_semantics=("arbitrary", "arbitrary", "arbitrary"),
            vmem_limit_bytes=VMEM_LIMIT_BYTES),
        name="diff_attn",
    )(lam, proj3d, proj3d, proj3d, subln_g_row)


def _mem_attention(qm, kv_ref, z_s, gate_m):
    for hd in range(MEM_HEADS):
        lo = hd * MEM_HEAD_DIM
        k = kv_ref[0, 0, :, lo:lo + MEM_HEAD_DIM]
        v = kv_ref[0, 0, :, MEM_WIDTH + lo:MEM_WIDTH + lo + MEM_HEAD_DIM]
        s = lax.dot_general(qm[:, lo:lo + MEM_HEAD_DIM], k, (((1,), (1,)), ((), ())),
                            preferred_element_type=F32) * (MEM_HEAD_DIM ** -0.5)
        p = jnp.exp(s - jnp.max(s, axis=-1, keepdims=True))
        l = jnp.sum(p, axis=-1, keepdims=True)
        m = jnp.dot(p.astype(BF16), v, preferred_element_type=F32) / l
        z_s[:, MIX_WIDTH + lo:MIX_WIDTH + lo + MEM_HEAD_DIM] = (
            m * _silu(gate_m[:, lo:lo + MEM_HEAD_DIM])).astype(BF16)


def _tail0_kernel(y_ref, gate_ref, qm_ref, h_ref, kv_ref, wo_ref, o_ref, z_s):
    gate = gate_ref[...].astype(F32)
    z_s[:, 0:MIX_WIDTH] = (y_ref[...].astype(F32) * _silu(gate[:, 0:MIX_WIDTH])).astype(BF16)
    _mem_attention(qm_ref[...], kv_ref, z_s, gate[:, MIX_WIDTH:])
    o_ref[...] = h_ref[...] + jnp.dot(z_s[...], wo_ref[...], preferred_element_type=F32)


def _tail0(y2d, proj2d, h2d, kv0, wo_bf16, *, seq, tm=512):
    t, d = h2d.shape
    nt = seq // tm
    return pl.pallas_call(
        _tail0_kernel,
        grid=(t // tm,),
        in_specs=[
            pl.BlockSpec((tm, MIX_WIDTH), lambda i: (i, 0)),
            pl.BlockSpec((tm, BRANCH_WIDTH), lambda i: (i, _GATE_OFF // BRANCH_WIDTH)),
            pl.BlockSpec((tm, MEM_WIDTH), lambda i: (i, _QM_OFF // MEM_WIDTH)),
            pl.BlockSpec((tm, d), lambda i: (i, 0)),
            pl.BlockSpec((1, 1) + kv0.shape[2:], lambda i: (0, i // nt, 0, 0)),
            pl.BlockSpec((BRANCH_WIDTH, d), lambda i: (0, 0)),
        ],
        out_specs=pl.BlockSpec((tm, d), lambda i: (i, 0)),
        out_shape=jax.ShapeDtypeStruct((t, d), F32),
        scratch_shapes=[pltpu.VMEM((tm, BRANCH_WIDTH), BF16)],
        compiler_params=pltpu.CompilerParams(
            dimension_semantics=("arbitrary",),
            vmem_limit_bytes=VMEM_LIMIT_BYTES),
        name="tail0",
    )(y2d, proj2d, proj2d, h2d, kv0, wo_bf16)


def _layer1_kernel(h_ref, g_ref, win_ref, wg_ref, ps_ref, kv_ref, wo_ref, fg_ref, o_ref,
                   u_s, z_s, *, tm):
    t_idx = pl.program_id(1)
    h = h_ref[...]
    hn = _rms(h, g_ref[...]).astype(BF16)

    @pl.when(t_idx == 0)
    def _():
        u_s[0:POOL_HALO, :] = jnp.zeros((POOL_HALO, MIX_WIDTH), F32)

    @pl.when(t_idx > 0)
    def _():
        u_s[0:POOL_HALO, :] = u_s[tm:tm + POOL_HALO, :]

    u_s[POOL_HALO:POOL_HALO + tm, :] = jnp.dot(hn, win_ref[:, 0:MIX_WIDTH],
                                               preferred_element_type=F32)
    qm = jnp.dot(hn, win_ref[:, MIX_WIDTH:MIX_WIDTH + MEM_WIDTH],
                 preferred_element_type=F32).astype(BF16)
    gate = jnp.dot(hn, win_ref[:, MIX_WIDTH + MEM_WIDTH:], preferred_element_type=F32)

    pos = t_idx * tm + lax.broadcasted_iota(jnp.int32, (tm, 1), 0)
    for gi, win in enumerate(POOL_WINDOWS):
        lo = gi * POOL_GROUP_WIDTH
        hi = lo + POOL_GROUP_WIDTH
        ug = u_s[POOL_HALO:POOL_HALO + tm, lo:hi]
        wsum = ug
        for back in range(1, win):
            wsum = wsum + u_s[POOL_HALO - back:POOL_HALO - back + tm, lo:hi]
        inv_cnt = 1.0 / jnp.minimum(pos + 1, win).astype(F32)
        pooled = (wsum * inv_cnt - ug).astype(BF16)
        mixed = jnp.dot(pooled, wg_ref[gi], preferred_element_type=F32) * ps_ref[:, lo:hi]
        z_s[:, lo:hi] = (mixed * _silu(gate[:, lo:hi])).astype(BF16)

    _mem_attention(qm, kv_ref, z_s, gate[:, MIX_WIDTH:])
    h2 = h + jnp.dot(z_s[...], wo_ref[...], preferred_element_type=F32)
    o_ref[...] = _rms(h2, fg_ref[...])


def _layer1(h2d, g_row, win_bf16, wg_bf16, ps_row, kv1, wo_bf16, fg_row, *, seq, tm=256):
    t, d = h2d.shape
    b = t // seq
    nt = seq // tm
    const2 = lambda i, j: (0, 0)
    return pl.pallas_call(
        functools.partial(_layer1_kernel, tm=tm),
        grid=(b, nt),
        in_specs=[
            pl.BlockSpec((tm, d), lambda i, j: (i * nt + j, 0)),
            pl.BlockSpec((1, d), const2),
            pl.BlockSpec((d, POOL_IN_WIDTH), const2),
            pl.BlockSpec(wg_bf16.shape, lambda i, j: (0, 0, 0)),
            pl.BlockSpec((1, MIX_WIDTH), const2),
            pl.BlockSpec((1, 1) + kv1.shape[2:], lambda i, j: (1, i, 0, 0)),
            pl.BlockSpec((BRANCH_WIDTH, d), const2),
            pl.BlockSpec((1, d), const2),
        ],
        out_specs=pl.BlockSpec((tm, d), lambda i, j: (i * nt + j, 0)),
        out_shape=jax.ShapeDtypeStruct((t, d), F32),
        scratch_shapes=[
            pltpu.VMEM((POOL_HALO + tm, MIX_WIDTH), F32),
            pltpu.VMEM((tm, BRANCH_WIDTH), BF16),
        ],
        compiler_params=pltpu.CompilerParams(
            dimension_semantics=("arbitrary", "arbitrary"),
            vmem_limit_bytes=VMEM_LIMIT_BYTES),
        name="layer1",
    )(h2d, g_row, win_bf16, wg_bf16, ps_row, kv1, wo_bf16, fg_row)


def kernel(x, mem, positions, ln_g, attn_w_in, attn_lambda, attn_subln_g, pool_w_in,
           pool_w_group, pool_scale, mem_norm_g, mem_w_kv, w_out, final_g):
    b, s, d = x.shape
    t = b * s
    lambda_init0 = 0.8 - 0.6 * math.exp(-0.3 * 0)

    w0 = attn_w_in[0]
    w0 = jnp.concatenate([w0[:, 3 * MIX_WIDTH + MEM_WIDTH:], w0[:, :3 * MIX_WIDTH + MEM_WIDTH]],
                         axis=1).astype(BF16)
    w1 = pool_w_in[0].astype(BF16)
    wg = pool_w_group[0].astype(BF16)
    wkv = mem_w_kv.astype(BF16)
    wo = w_out.astype(BF16)
    inv = ROPE_THETA ** (-jnp.arange(0, DA_QK_DIM, 2, dtype=F32) / DA_QK_DIM)
    inv_row = jnp.tile(inv, LANES // inv.shape[0]).reshape(1, LANES)

    h2d = x.reshape(t, d)
    kv = _mem_kv(mem, mem_norm_g, wkv)

    proj = _inproj0(h2d, positions.reshape(t, 1), inv_row, ln_g[0].reshape(1, d), w0)
    y = _diff_attn(proj.reshape(b, s, ATTN_IN_WIDTH), attn_lambda[0:1],
                   attn_subln_g[0].reshape(1, DA_V_DIM), lambda_init0)
    h1 = _tail0(y.reshape(t, MIX_WIDTH), proj, h2d, kv, wo[0], seq=s)
    out = _layer1(h1, ln_g[1].reshape(1, d), w1, wg, pool_scale[0].reshape(1, MIX_WIDTH),
                  kv, wo[1], final_g.reshape(1, d), seq=s)
    return out.reshape(b, s, d)
```

```python
import functools
import math

import jax
import jax.numpy as jnp
from jax import lax
from jax.experimental import pallas as pl
from jax.experimental.pallas import tpu as pltpu

F32 = jnp.float32
BF16 = jnp.bfloat16

D_MODEL = 1024
MEM_HEADS = 4
MEM_HEAD_DIM = 128
MEM_WIDTH = MEM_HEADS * MEM_HEAD_DIM
BRANCH_WIDTH = 2 * D_MODEL
MIX_WIDTH = BRANCH_WIDTH - MEM_WIDTH
DA_QK_DIM = 64
DA_V_DIM = 2 * DA_QK_DIM
DA_HEADS = MIX_WIDTH // DA_V_DIM
ROPE_THETA = 10000.0
POOL_WINDOWS = (2, 4, 8, 16)
POOL_GROUP_WIDTH = MIX_WIDTH // len(POOL_WINDOWS)
POOL_HALO = 16
EPS = 1e-6
ATTN_IN_WIDTH = 3 * MIX_WIDTH + MEM_WIDTH + BRANCH_WIDTH
POOL_IN_WIDTH = MIX_WIDTH + MEM_WIDTH + BRANCH_WIDTH

LANES = 128
VMEM_LIMIT_BYTES = 56 * 1024 * 1024

_GATE_OFF = 0
_Q_OFF = BRANCH_WIDTH
_K_OFF = _Q_OFF + MIX_WIDTH
_V_OFF = _K_OFF + MIX_WIDTH
_QM_OFF = _V_OFF + MIX_WIDTH


def _rms(x, g):
    ms = jnp.mean(x * x, axis=-1, keepdims=True)
    return x * lax.rsqrt(ms + EPS) * g


def _silu(x):
    hx = 0.5 * x
    return hx + hx * jnp.tanh(hx)


def _mem_kv_kernel(mem_ref, g_ref, w_ref, o_ref):
    mem_n = _rms(mem_ref[0], g_ref[...]).astype(BF16)
    o_ref[0, 0] = jnp.dot(mem_n, w_ref[0], preferred_element_type=F32).astype(BF16)


def _mem_kv(mem, mem_norm_g, w_kv_bf16):
    depth = w_kv_bf16.shape[0]
    b, m, d = mem.shape
    return pl.pallas_call(
        _mem_kv_kernel,
        grid=(depth, b),
        in_specs=[
            pl.BlockSpec((1, m, d), lambda l, i: (i, 0, 0)),
            pl.BlockSpec((1, d), lambda l, i: (0, 0)),
            pl.BlockSpec((1, d, 2 * MEM_WIDTH), lambda l, i: (l, 0, 0)),
        ],
        out_specs=pl.BlockSpec((1, 1, m, 2 * MEM_WIDTH), lambda l, i: (l, i, 0, 0)),
        out_shape=jax.ShapeDtypeStruct((depth, b, m, 2 * MEM_WIDTH), BF16),
        compiler_params=pltpu.CompilerParams(vmem_limit_bytes=VMEM_LIMIT_BYTES),
        name="mem_kv",
    )(mem, mem_norm_g.reshape(1, d), w_kv_bf16)


def _inproj0_kernel(pos_ref, inv_ref, x_ref, g_ref, w_ref, o_ref, hn_s, cos_s, sin_s, *, tn):
    lane = lax.broadcasted_iota(jnp.int32, (1, LANES), 1)
    first_half = (lane % DA_QK_DIM) < (DA_QK_DIM // 2)
    hn_s[...] = _rms(x_ref[...], g_ref[...]).astype(BF16)
    ang = pos_ref[...].astype(F32) * inv_ref[...]
    cos_s[...] = jnp.cos(ang)
    sn = jnp.sin(ang)
    sin_s[...] = jnp.where(first_half, -sn, sn)

    qscale = DA_QK_DIM ** -0.5 * math.log2(math.e)
    for lo in range(0, w_ref.shape[1], tn):
        acc = jnp.dot(hn_s[...], w_ref[:, lo:lo + tn], preferred_element_type=F32)
        if _Q_OFF <= lo < _V_OFF:
            scale = qscale if lo < _K_OFF else 1.0
            cos = cos_s[...] * scale
            sin = sin_s[...] * scale
            for c in range(0, tn, LANES):
                xc = acc[:, c:c + LANES]
                rot = jnp.where(first_half,
                                pltpu.roll(xc, LANES - DA_QK_DIM // 2, 1),
                                pltpu.roll(xc, DA_QK_DIM // 2, 1))
                o_ref[:, lo + c:lo + c + LANES] = (xc * cos + rot * sin).astype(BF16)
        else:
            o_ref[:, lo:lo + tn] = acc.astype(BF16)


def _inproj0(h2d, pos2d, inv_row, g_row, w_bf16, *, tm=512, tn=512):
    t, d = h2d.shape
    n = w_bf16.shape[1]
    return pl.pallas_call(
        functools.partial(_inproj0_kernel, tn=tn),
        grid=(t // tm,),
        in_specs=[
            pl.BlockSpec((tm, 1), lambda i: (i, 0)),
            pl.BlockSpec((1, LANES), lambda i: (0, 0)),
            pl.BlockSpec((tm, d), lambda i: (i, 0)),
            pl.BlockSpec((1, d), lambda i: (0, 0)),
            pl.BlockSpec((d, n), lambda i: (0, 0), pipeline_mode=pl.Buffered(1)),
        ],
        out_specs=pl.BlockSpec((tm, n), lambda i: (i, 0)),
        out_shape=jax.ShapeDtypeStruct((t, n), BF16),
        scratch_shapes=[
            pltpu.VMEM((tm, d), BF16),
            pltpu.VMEM((tm, LANES), F32),
            pltpu.VMEM((tm, LANES), F32),
        ],
        compiler_params=pltpu.CompilerParams(
            dimension_semantics=("arbitrary",),
            vmem_limit_bytes=VMEM_LIMIT_BYTES),
        name="inproj0",
    )(pos2d, inv_row, h2d, g_row, w_bf16)


def _diff_attn_kernel(lam_ref, q_ref, k_ref, v_ref, g_ref, o_ref,
                      qs_s, vx_s, s_s, m_s, acc_s, *, tq, rc, nq, lambda_init):
    qi = pl.program_id(2)
    chunks = range(0, 2 * tq, rc)

    def stack_q(start):
        q = q_ref[0, pl.ds(start, tq), :]
        lane = lax.broadcasted_iota(jnp.int32, (tq, LANES), 1)
        zero = jnp.zeros_like(q)
        qs_s[0:tq, :] = jnp.where(lane < DA_QK_DIM, q, zero)
        qs_s[tq:2 * tq, :] = jnp.where(lane >= DA_QK_DIM, q, zero)

    def reset_state():
        m_s[...] = jnp.full(m_s.shape, -jnp.inf, F32)
        acc_s[...] = jnp.zeros(acc_s.shape, F32)

    def diag_cols(r):
        return r % tq + rc

    def put_scores(r, start, ncols):
        k = k_ref[0, pl.ds(start, ncols), :]
        s_s[r:r + rc, 0:ncols] = lax.dot_general(qs_s[r:r + rc, :], k, (((1,), (1,)), ((), ())),
                                                 preferred_element_type=F32)

    def update(r, s, start, ncols, masked):
        vx = vx_s[pl.ds(start, ncols), :]
        cols = [s[:, c:c + LANES] for c in range(0, ncols, LANES)]
        if masked:
            row = lax.broadcasted_iota(jnp.int32, (rc, LANES), 0)
            col = lax.broadcasted_iota(jnp.int32, (rc, LANES), 1)
            for i in range(rc // LANES):
                ci = (ncols - rc) // LANES + i
                cols[ci] = jnp.where(col + i * LANES <= row, cols[ci], -jnp.inf)
        m_prev = m_s[r:r + rc, :]
        m_cur = cols[0]
        for c in cols[1:]:
            m_cur = jnp.maximum(m_cur, c)
        m_new = jnp.maximum(m_prev, jnp.max(m_cur, axis=-1, keepdims=True))
        alpha = jnp.exp2(m_prev - m_new)
        p = jnp.concatenate([jnp.exp2(c - m_new) for c in cols], axis=1).astype(BF16)
        pv = jnp.dot(p, vx, preferred_element_type=F32)
        acc_s[r:r + rc, :] = jnp.concatenate([alpha, alpha], axis=1) * acc_s[r:r + rc, :] + pv
        m_s[r:r + rc, :] = m_new

    @pl.when(qi == 0)
    def _():
        vx_s[:, 0:LANES] = v_ref[0]
        vx_s[:, LANES:2 * LANES] = jnp.ones((vx_s.shape[0], LANES), BF16)
        reset_state()
        stack_q(0)
        for r in chunks:
            put_scores(r, 0, diag_cols(r))

    def full_stage(kt, next_is_diag):
        start = pl.multiple_of(kt * tq, tq)
        for r in chunks:
            s = s_s[r:r + rc, :]
            put_scores(r, start + tq, diag_cols(r) if next_is_diag else tq)
            update(r, s, start, tq, False)

    def body(kt, carry):
        full_stage(kt, False)
        return carry

    lax.fori_loop(0, qi - 1, body, 0)
    pl.when(qi > 0)(lambda: full_stage(qi - 1, True))

    def diagonal_stage(has_next):
        diag = pl.multiple_of(qi * tq, tq)
        if has_next:
            stack_q(diag + tq)
        for r in chunks:
            ncols = diag_cols(r)
            s = s_s[r:r + rc, 0:ncols]
            if has_next:
                put_scores(r, 0, tq)
            update(r, s, diag, ncols, True)
        lf = lam_ref[0]
        lam_full = (jnp.exp(jnp.sum(lf[0:1] * lf[1:2], axis=-1, keepdims=True))
                    - jnp.exp(jnp.sum(lf[2:3] * lf[3:4], axis=-1, keepdims=True))
                    + lambda_init)
        o = (acc_s[0:tq, 0:LANES] / acc_s[0:tq, LANES:2 * LANES]
             - lam_full * (acc_s[tq:2 * tq, 0:LANES] / acc_s[tq:2 * tq, LANES:2 * LANES]))
        o_ref[0] = (_rms(o, g_ref[...]) * (1.0 - lambda_init)).astype(BF16)
        reset_state()

    pl.when(qi < nq - 1)(functools.partial(diagonal_stage, True))
    pl.when(qi == nq - 1)(functools.partial(diagonal_stage, False))


def _diff_attn(proj3d, lam, subln_g_row, lambda_init, *, tq=1024, rc=256):
    b, s, _ = proj3d.shape
    qb, kb, vb = _Q_OFF // LANES, _K_OFF // LANES, _V_OFF // LANES
    return pl.pallas_call(
        functools.partial(_diff_attn_kernel, tq=tq, rc=rc, nq=s // tq, lambda_init=lambda_init),
        grid=(b, DA_HEADS, s // tq),
        in_specs=[
            pl.BlockSpec((1, 4, DA_QK_DIM), lambda i, h, t: (0, 0, 0)),
            pl.BlockSpec((1, s, LANES), lambda i, h, t: (i, 0, qb + h)),
            pl.BlockSpec((1, s, LANES), lambda i, h, t: (i, 0, kb + h)),
            pl.BlockSpec((1, s, LANES), lambda i, h, t: (i, 0, vb + h)),
            pl.BlockSpec((1, LANES), lambda i, h, t: (0, 0)),
        ],
        out_specs=pl.BlockSpec((1, tq, LANES), lambda i, h, t: (i, t, h)),
        out_shape=jax.ShapeDtypeStruct((b, s, MIX_WIDTH), BF16),
        scratch_shapes=[
            pltpu.VMEM((2 * tq, LANES), BF16),
            pltpu.VMEM((s, 2 * LANES), BF16),
            pltpu.VMEM((2 * tq, tq), F32),
            pltpu.VMEM((2 * tq, LANES), F32),
            pltpu.VMEM((2 * tq, 2 * LANES), F32),
        ],
        compiler_params=pltpu.CompilerParams(
            dimension_semantics=("arbitrary", "arbitrary", "arbitrary"),
            vmem_limit_bytes=VMEM_LIMIT_BYTES),
        name="diff_attn",
    )(lam, proj3d, proj3d, proj3d, subln_g_row)


def _mem_attention(qm, kv_ref, z_s, gate_m, rows):
    for hd in range(MEM_HEADS):
        lo = hd * MEM_HEAD_DIM
        k = kv_ref[0, 0, :, lo:lo + MEM_HEAD_DIM]
        v = kv_ref[0, 0, :, MEM_WIDTH + lo:MEM_WIDTH + lo + MEM_HEAD_DIM]
        s = lax.dot_general(qm[:, lo:lo + MEM_HEAD_DIM], k, (((1,), (1,)), ((), ())),
                            preferred_element_type=F32) * (MEM_HEAD_DIM ** -0.5)
        p = jnp.exp(s - jnp.max(s, axis=-1, keepdims=True))
        l = jnp.sum(p, axis=-1, keepdims=True)
        m = jnp.dot(p.astype(BF16), v, preferred_element_type=F32) / l
        z_s[rows, MIX_WIDTH + lo:MIX_WIDTH + lo + MEM_HEAD_DIM] = (
            m * _silu(gate_m[:, lo:lo + MEM_HEAD_DIM])).astype(BF16)


def _tail0_kernel(y_ref, gate_ref, qm_ref, h_ref, kv_ref, wo_ref, o_ref, z_s, *, sub):
    for r0 in range(0, z_s.shape[0], sub):
        rows = pl.ds(r0, sub)
        gate = gate_ref[rows, :].astype(F32)
        z_s[rows, 0:MIX_WIDTH] = (y_ref[rows, :].astype(F32)
                                  * _silu(gate[:, 0:MIX_WIDTH])).astype(BF16)
        _mem_attention(qm_ref[rows, :], kv_ref, z_s, gate[:, MIX_WIDTH:], rows)
        o_ref[rows, :] = h_ref[rows, :] + jnp.dot(z_s[rows, :], wo_ref[...],
                                                  preferred_element_type=F32)


def _tail0(y2d, proj2d, h2d, kv0, wo_bf16, *, seq, tm=512, sub=256):
    t, d = h2d.shape
    nt = seq // tm
    return pl.pallas_call(
        functools.partial(_tail0_kernel, sub=sub),
        grid=(t // tm,),
        in_specs=[
            pl.BlockSpec((tm, MIX_WIDTH), lambda i: (i, 0)),
            pl.BlockSpec((tm, BRANCH_WIDTH), lambda i: (i, _GATE_OFF // BRANCH_WIDTH)),
            pl.BlockSpec((tm, MEM_WIDTH), lambda i: (i, _QM_OFF // MEM_WIDTH)),
            pl.BlockSpec((tm, d), lambda i: (i, 0)),
            pl.BlockSpec((1, 1) + kv0.shape[2:], lambda i: (0, i // nt, 0, 0)),
            pl.BlockSpec((BRANCH_WIDTH, d), lambda i: (0, 0), pipeline_mode=pl.Buffered(1)),
        ],
        out_specs=pl.BlockSpec((tm, d), lambda i: (i, 0)),
        out_shape=jax.ShapeDtypeStruct((t, d), F32),
        scratch_shapes=[pltpu.VMEM((tm, BRANCH_WIDTH), BF16)],
        compiler_params=pltpu.CompilerParams(
            dimension_semantics=("arbitrary",),
            vmem_limit_bytes=VMEM_LIMIT_BYTES),
        name="tail0",
    )(y2d, proj2d, proj2d, h2d, kv0, wo_bf16)


def _layer1_kernel(h_ref, g_ref, win_ref, wg_ref, ps_ref, kv_ref, wo_ref, fg_ref, o_ref,
                   u_s, z_s, *, tm, sub):
    t_idx = pl.program_id(1)

    @pl.when(t_idx == 0)
    def _():
        u_s[0:POOL_HALO, :] = jnp.zeros((POOL_HALO, MIX_WIDTH), F32)

    @pl.when(t_idx > 0)
    def _():
        u_s[0:POOL_HALO, :] = u_s[tm:tm + POOL_HALO, :]

    for r0 in range(0, tm, sub):
        rows = pl.ds(r0, sub)
        h = h_ref[rows, :]
        hn = _rms(h, g_ref[...]).astype(BF16)
        u0 = POOL_HALO + r0
        u_s[u0:u0 + sub, :] = jnp.dot(hn, win_ref[:, 0:MIX_WIDTH], preferred_element_type=F32)
        qm = jnp.dot(hn, win_ref[:, MIX_WIDTH:MIX_WIDTH + MEM_WIDTH],
                     preferred_element_type=F32).astype(BF16)
        gate = jnp.dot(hn, win_ref[:, MIX_WIDTH + MEM_WIDTH:], preferred_element_type=F32)

        pos = t_idx * tm + r0 + lax.broadcasted_iota(jnp.int32, (sub, 1), 0)
        for gi, win in enumerate(POOL_WINDOWS):
            lo = gi * POOL_GROUP_WIDTH
            hi = lo + POOL_GROUP_WIDTH
            w = u_s[u0 - POOL_HALO:u0 + sub, lo:hi]
            ug = w[POOL_HALO:, :]
            step = 1
            while step < win:
                w = w + pltpu.roll(w, step, 0)
                step *= 2
            wsum = w[POOL_HALO:, :]
            inv_cnt = 1.0 / jnp.minimum(pos + 1, win).astype(F32)
            pooled = (wsum * inv_cnt - ug).astype(BF16)
            mixed = jnp.dot(pooled, wg_ref[gi], preferred_element_type=F32) * ps_ref[:, lo:hi]
            z_s[rows, lo:hi] = (mixed * _silu(gate[:, lo:hi])).astype(BF16)

        _mem_attention(qm, kv_ref, z_s, gate[:, MIX_WIDTH:], rows)
        h2 = h + jnp.dot(z_s[rows, :], wo_ref[...], preferred_element_type=F32)
        o_ref[rows, :] = _rms(h2, fg_ref[...])


def _layer1(h2d, g_row, win_bf16, wg_bf16, ps_row, kv1, wo_bf16, fg_row, *, seq, tm=512, sub=256):
    t, d = h2d.shape
    b = t // seq
    nt = seq // tm
    const2 = lambda i, j: (0, 0)
    resident = dict(pipeline_mode=pl.Buffered(1))
    return pl.pallas_call(
        functools.partial(_layer1_kernel, tm=tm, sub=sub),
        grid=(b, nt),
        in_specs=[
            pl.BlockSpec((tm, d), lambda i, j: (i * nt + j, 0)),
            pl.BlockSpec((1, d), const2),
            pl.BlockSpec((d, POOL_IN_WIDTH), const2, **resident),
            pl.BlockSpec(wg_bf16.shape, lambda i, j: (0, 0, 0), **resident),
            pl.BlockSpec((1, MIX_WIDTH), const2),
            pl.BlockSpec((1, 1) + kv1.shape[2:], lambda i, j: (1, i, 0, 0)),
            pl.BlockSpec((BRANCH_WIDTH, d), const2, **resident),
            pl.BlockSpec((1, d), const2),
        ],
        out_specs=pl.BlockSpec((tm, d), lambda i, j: (i * nt + j, 0)),
        out_shape=jax.ShapeDtypeStruct((t, d), F32),
        scratch_shapes=[
            pltpu.VMEM((POOL_HALO + tm, MIX_WIDTH), F32),
            pltpu.VMEM((tm, BRANCH_WIDTH), BF16),
        ],
        compiler_params=pltpu.CompilerParams(
            dimension_semantics=("arbitrary", "arbitrary"),
            vmem_limit_bytes=VMEM_LIMIT_BYTES),
        name="layer1",
    )(h2d, g_row, win_bf16, wg_bf16, ps_row, kv1, wo_bf16, fg_row)


def kernel(x, mem, positions, ln_g, attn_w_in, attn_lambda, attn_subln_g, pool_w_in,
           pool_w_group, pool_scale, mem_norm_g, mem_w_kv, w_out, final_g):
    b, s, d = x.shape
    t = b * s
    lambda_init0 = 0.8 - 0.6 * math.exp(-0.3 * 0)

    w0 = attn_w_in[0]
    w0 = jnp.concatenate([w0[:, 3 * MIX_WIDTH + MEM_WIDTH:], w0[:, :3 * MIX_WIDTH + MEM_WIDTH]],
                         axis=1).astype(BF16)
    w1 = pool_w_in[0].astype(BF16)
    wg = pool_w_group[0].astype(BF16)
    wkv = mem_w_kv.astype(BF16)
    wo = w_out.astype(BF16)
    inv = ROPE_THETA ** (-jnp.arange(0, DA_QK_DIM, 2, dtype=F32) / DA_QK_DIM)
    inv_row = jnp.tile(inv, LANES // inv.shape[0]).reshape(1, LANES)

    h2d = x.reshape(t, d)
    kv = _mem_kv(mem, mem_norm_g, wkv)

    proj = _inproj0(h2d, positions.reshape(t, 1), inv_row, ln_g[0].reshape(1, d), w0)
    y = _diff_attn(proj.reshape(b, s, ATTN_IN_WIDTH), attn_lambda[0:1],
                   attn_subln_g[0].reshape(1, DA_V_DIM), lambda_init0)
    h1 = _tail0(y.reshape(t, MIX_WIDTH), proj, h2d, kv, wo[0], seq=s)
    out = _layer1(h1, ln_g[1].reshape(1, d), w1, wg, pool_scale[0].reshape(1, MIX_WIDTH),
                  kv, wo[1], final_g.reshape(1, d), seq=s)
    return out.reshape(b, s, d)
```

```python
import functools
import math

import jax
import jax.numpy as jnp
from jax import lax
from jax.experimental import pallas as pl
from jax.experimental.pallas import tpu as pltpu

F32 = jnp.float32
BF16 = jnp.bfloat16

D_MODEL = 1024
MEM_HEADS = 4
MEM_HEAD_DIM = 128
MEM_WIDTH = MEM_HEADS * MEM_HEAD_DIM
BRANCH_WIDTH = 2 * D_MODEL
MIX_WIDTH = BRANCH_WIDTH - MEM_WIDTH
DA_QK_DIM = 64
DA_V_DIM = 2 * DA_QK_DIM
DA_HEADS = MIX_WIDTH // DA_V_DIM
ROPE_THETA = 10000.0
POOL_WINDOWS = (2, 4, 8, 16)
POOL_GROUP_WIDTH = MIX_WIDTH // len(POOL_WINDOWS)
POOL_HALO = 16
EPS = 1e-6
ATTN_IN_WIDTH = 3 * MIX_WIDTH + MEM_WIDTH + BRANCH_WIDTH
POOL_IN_WIDTH = MIX_WIDTH + MEM_WIDTH + BRANCH_WIDTH

LANES = 128
VMEM_LIMIT_BYTES = 56 * 1024 * 1024

_GATE_OFF = 0
_Q_OFF = BRANCH_WIDTH
_K_OFF = _Q_OFF + MIX_WIDTH
_QM_OFF = _K_OFF + MIX_WIDTH
_PROJ_WIDTH = _QM_OFF + MEM_WIDTH
VT_TILE = 512


def _rms(x, g):
    ms = jnp.mean(x * x, axis=-1, keepdims=True)
    return x * lax.rsqrt(ms + EPS) * g


def _silu(x):
    hx = 0.5 * x
    return hx + hx * jnp.tanh(hx)


def _mem_kv_kernel(mem_ref, g_ref, w_ref, o_ref):
    mem_n = _rms(mem_ref[0], g_ref[...]).astype(BF16)
    o_ref[0, 0] = jnp.dot(mem_n, w_ref[0], preferred_element_type=F32).astype(BF16)


def _mem_kv(mem, mem_norm_g, w_kv_bf16):
    depth = w_kv_bf16.shape[0]
    b, m, d = mem.shape
    return pl.pallas_call(
        _mem_kv_kernel,
        grid=(depth, b),
        in_specs=[
            pl.BlockSpec((1, m, d), lambda l, i: (i, 0, 0)),
            pl.BlockSpec((1, d), lambda l, i: (0, 0)),
            pl.BlockSpec((1, d, 2 * MEM_WIDTH), lambda l, i: (l, 0, 0)),
        ],
        out_specs=pl.BlockSpec((1, 1, m, 2 * MEM_WIDTH), lambda l, i: (l, i, 0, 0)),
        out_shape=jax.ShapeDtypeStruct((depth, b, m, 2 * MEM_WIDTH), BF16),
        compiler_params=pltpu.CompilerParams(vmem_limit_bytes=VMEM_LIMIT_BYTES),
        name="mem_kv",
    )(mem, mem_norm_g.reshape(1, d), w_kv_bf16)


def _inproj0_kernel(pos_ref, inv_ref, x_ref, g_ref, w_ref, wvt_ref, o_ref, vt_ref,
                    hn_s, cos_s, sin_s, *, tn):
    lane = lax.broadcasted_iota(jnp.int32, (1, LANES), 1)
    first_half = (lane % DA_QK_DIM) < (DA_QK_DIM // 2)
    hn_s[...] = _rms(x_ref[...], g_ref[...]).astype(BF16)
    ang = pos_ref[...].astype(F32) * inv_ref[...]
    cos_s[...] = jnp.cos(ang)
    sn = jnp.sin(ang)
    sin_s[...] = jnp.where(first_half, -sn, sn)

    qscale = DA_QK_DIM ** -0.5 * math.log2(math.e)
    for lo in range(0, w_ref.shape[1], tn):
        acc = jnp.dot(hn_s[...], w_ref[:, lo:lo + tn], preferred_element_type=F32)
        if _Q_OFF <= lo < _QM_OFF:
            scale = qscale if lo < _K_OFF else 1.0
            cos = cos_s[...] * scale
            sin = sin_s[...] * scale
            for c in range(0, tn, LANES):
                xc = acc[:, c:c + LANES]
                rot = jnp.where(first_half,
                                pltpu.roll(xc, LANES - DA_QK_DIM // 2, 1),
                                pltpu.roll(xc, DA_QK_DIM // 2, 1))
                o_ref[:, lo + c:lo + c + LANES] = (xc * cos + rot * sin).astype(BF16)
        else:
            o_ref[:, lo:lo + tn] = acc.astype(BF16)

    for lo in range(0, wvt_ref.shape[0], tn):
        vt_ref[0, 0, lo:lo + tn, :] = lax.dot_general(
            wvt_ref[lo:lo + tn, :], hn_s[...], (((1,), (1,)), ((), ())),
            preferred_element_type=F32).astype(BF16)


def _inproj0(h2d, pos2d, inv_row, g_row, w_bf16, wvt_bf16, *, seq, tn=512):
    t, d = h2d.shape
    n = w_bf16.shape[1]
    tm = VT_TILE
    nt = seq // tm
    resident = dict(pipeline_mode=pl.Buffered(1))
    return pl.pallas_call(
        functools.partial(_inproj0_kernel, tn=tn),
        grid=(t // tm,),
        in_specs=[
            pl.BlockSpec((tm, 1), lambda i: (i, 0)),
            pl.BlockSpec((1, LANES), lambda i: (0, 0)),
            pl.BlockSpec((tm, d), lambda i: (i, 0)),
            pl.BlockSpec((1, d), lambda i: (0, 0)),
            pl.BlockSpec((d, n), lambda i: (0, 0), **resident),
            pl.BlockSpec((MIX_WIDTH, d), lambda i: (0, 0), **resident),
        ],
        out_specs=[
            pl.BlockSpec((tm, n), lambda i: (i, 0)),
            pl.BlockSpec((1, 1, MIX_WIDTH, tm), lambda i: (i // nt, i % nt, 0, 0)),
        ],
        out_shape=[
            jax.ShapeDtypeStruct((t, n), BF16),
            jax.ShapeDtypeStruct((t // seq, nt, MIX_WIDTH, tm), BF16),
        ],
        scratch_shapes=[
            pltpu.VMEM((tm, d), BF16),
            pltpu.VMEM((tm, LANES), F32),
            pltpu.VMEM((tm, LANES), F32),
        ],
        compiler_params=pltpu.CompilerParams(
            dimension_semantics=("arbitrary",),
            vmem_limit_bytes=VMEM_LIMIT_BYTES),
        name="inproj0",
    )(pos2d, inv_row, h2d, g_row, w_bf16, wvt_bf16)


def _diff_attn_kernel(lam_ref, q_ref, k_ref, vt_ref, g_ref, o_ref,
                      qs_s, s_s, m_s, l_s, acc_s, *, tq, cq, nq, lambda_init):
    qi = pl.program_id(2)
    chunks = range(0, 2 * tq, cq)

    def stack_q(start):
        q = q_ref[0, pl.ds(start, tq), :]
        lane = lax.broadcasted_iota(jnp.int32, (tq, LANES), 1)
        zero = jnp.zeros_like(q)
        qs_s[0:tq, :] = jnp.where(lane < DA_QK_DIM, q, zero)
        qs_s[tq:2 * tq, :] = jnp.where(lane >= DA_QK_DIM, q, zero)

    def reset_state():
        m_s[...] = jnp.full(m_s.shape, -jnp.inf, F32)
        l_s[...] = jnp.zeros(l_s.shape, F32)
        acc_s[...] = jnp.zeros(acc_s.shape, F32)

    def diag_keys(c):
        return c % tq + cq

    def put_scores(c, start, nk):
        k = k_ref[0, pl.ds(start, nk), :]
        s_s[0:nk, c:c + cq] = lax.dot_general(k, qs_s[c:c + cq, :], (((1,), (1,)), ((), ())),
                                              preferred_element_type=F32)

    def update(c, s, kt, nk, masked):
        if masked:
            key = lax.broadcasted_iota(jnp.int32, (cq, cq), 0)
            qry = lax.broadcasted_iota(jnp.int32, (cq, cq), 1)
            tail = jnp.where(key <= qry, s[nk - cq:nk, :], -jnp.inf)
            s = tail if nk == cq else jnp.concatenate([s[0:nk - cq, :], tail], axis=0)
        m_prev = m_s[:, c:c + cq]
        m_new = jnp.maximum(m_prev, jnp.max(s, axis=0, keepdims=True))
        alpha = jnp.exp2(m_prev - m_new)
        p = jnp.exp2(s - m_new)
        l_s[:, c:c + cq] = alpha * l_s[:, c:c + cq] + jnp.sum(p, axis=0, keepdims=True)
        pb = p.astype(BF16)
        pv = None
        for j, k0 in enumerate(range(0, nk, VT_TILE)):
            n = min(VT_TILE, nk - k0)
            part = jnp.dot(vt_ref[0, (tq // VT_TILE) * kt + j, :, 0:n], pb[k0:k0 + n, :],
                           preferred_element_type=F32)
            pv = part if pv is None else pv + part
        acc_s[:, c:c + cq] = alpha * acc_s[:, c:c + cq] + pv
        m_s[:, c:c + cq] = m_new

    @pl.when(qi == 0)
    def _():
        reset_state()
        stack_q(0)
        for c in chunks:
            put_scores(c, 0, diag_keys(c))

    def full_stage(kt, next_is_diag):
        start = pl.multiple_of(kt * tq, tq)
        for c in chunks:
            s = s_s[:, c:c + cq]
            put_scores(c, start + tq, diag_keys(c) if next_is_diag else tq)
            update(c, s, kt, tq, False)

    def body(kt, carry):
        full_stage(kt, False)
        return carry

    lax.fori_loop(0, qi - 1, body, 0)
    pl.when(qi > 0)(lambda: full_stage(qi - 1, True))

    def diagonal_stage(has_next):
        if has_next:
            stack_q(pl.multiple_of(qi * tq, tq) + tq)
        for c in chunks:
            nk = diag_keys(c)
            s = s_s[0:nk, c:c + cq]
            if has_next:
                put_scores(c, 0, tq)
            update(c, s, qi, nk, True)
        lf = lam_ref[0]
        lam_full = (jnp.exp(jnp.sum(lf[0:1] * lf[1:2], axis=-1, keepdims=True))
                    - jnp.exp(jnp.sum(lf[2:3] * lf[3:4], axis=-1, keepdims=True))
                    + lambda_init)
        ot = (acc_s[:, 0:tq] / l_s[:, 0:tq]
              - lam_full * (acc_s[:, tq:2 * tq] / l_s[:, tq:2 * tq]))
        ms = jnp.mean(ot * ot, axis=0, keepdims=True)
        g = jnp.concatenate([g_ref[...]] * (tq // LANES), axis=1)
        ot = ot * lax.rsqrt(ms + EPS) * g * (1.0 - lambda_init)
        o_ref[0] = ot.T.astype(BF16)
        reset_state()

    pl.when(qi < nq - 1)(functools.partial(diagonal_stage, True))
    pl.when(qi == nq - 1)(functools.partial(diagonal_stage, False))


def _diff_attn(proj3d, vt4d, lam, subln_g_sq, lambda_init, *, tq=1024, cq=256):
    b, s, _ = proj3d.shape
    qb, kb = _Q_OFF // LANES, _K_OFF // LANES
    return pl.pallas_call(
        functools.partial(_diff_attn_kernel, tq=tq, cq=cq, nq=s // tq, lambda_init=lambda_init),
        grid=(b, DA_HEADS, s // tq),
        in_specs=[
            pl.BlockSpec((1, 4, DA_QK_DIM), lambda i, h, t: (0, 0, 0)),
            pl.BlockSpec((1, s, LANES), lambda i, h, t: (i, 0, qb + h)),
            pl.BlockSpec((1, s, LANES), lambda i, h, t: (i, 0, kb + h)),
            pl.BlockSpec((1, s // VT_TILE, DA_V_DIM, VT_TILE), lambda i, h, t: (i, 0, h, 0)),
            pl.BlockSpec((DA_V_DIM, LANES), lambda i, h, t: (0, 0)),
        ],
        out_specs=pl.BlockSpec((1, tq, LANES), lambda i, h, t: (i, t, h)),
        out_shape=jax.ShapeDtypeStruct((b, s, MIX_WIDTH), BF16),
        scratch_shapes=[
            pltpu.VMEM((2 * tq, LANES), BF16),
            pltpu.VMEM((tq, 2 * tq), F32),
            pltpu.VMEM((1, 2 * tq), F32),
            pltpu.VMEM((1, 2 * tq), F32),
            pltpu.VMEM((DA_V_DIM, 2 * tq), F32),
        ],
        compiler_params=pltpu.CompilerParams(
            dimension_semantics=("arbitrary", "arbitrary", "arbitrary"),
            vmem_limit_bytes=VMEM_LIMIT_BYTES),
        name="diff_attn",
    )(lam, proj3d, proj3d, vt4d, subln_g_sq)


def _mem_attention(qm, kv_ref, z_s, gate_m, rows):
    for hd in range(MEM_HEADS):
        lo = hd * MEM_HEAD_DIM
        k = kv_ref[0, 0, :, lo:lo + MEM_HEAD_DIM]
        v = kv_ref[0, 0, :, MEM_WIDTH + lo:MEM_WIDTH + lo + MEM_HEAD_DIM]
        s = lax.dot_general(qm[:, lo:lo + MEM_HEAD_DIM], k, (((1,), (1,)), ((), ())),
                            preferred_element_type=F32) * (MEM_HEAD_DIM ** -0.5)
        p = jnp.exp(s - jnp.max(s, axis=-1, keepdims=True))
        l = jnp.sum(p, axis=-1, keepdims=True)
        m = jnp.dot(p.astype(BF16), v, preferred_element_type=F32) / l
        z_s[rows, MIX_WIDTH + lo:MIX_WIDTH + lo + MEM_HEAD_DIM] = (
            m * _silu(gate_m[:, lo:lo + MEM_HEAD_DIM])).astype(BF16)


def _tail0_kernel(y_ref, gate_ref, qm_ref, h_ref, kv_ref, wo_ref, o_ref, z_s, *, sub):
    for r0 in range(0, z_s.shape[0], sub):
        rows = pl.ds(r0, sub)
        gate = gate_ref[rows, :].astype(F32)
        z_s[rows, 0:MIX_WIDTH] = (y_ref[rows, :].astype(F32)
                                  * _silu(gate[:, 0:MIX_WIDTH])).astype(BF16)
        _mem_attention(qm_ref[rows, :], kv_ref, z_s, gate[:, MIX_WIDTH:], rows)
        o_ref[rows, :] = h_ref[rows, :] + jnp.dot(z_s[rows, :], wo_ref[...],
                                                  preferred_element_type=F32)


def _tail0(y2d, proj2d, h2d, kv0, wo_bf16, *, seq, tm=512, sub=256):
    t, d = h2d.shape
    nt = seq // tm
    return pl.pallas_call(
        functools.partial(_tail0_kernel, sub=sub),
        grid=(t // tm,),
        in_specs=[
            pl.BlockSpec((tm, MIX_WIDTH), lambda i: (i, 0)),
            pl.BlockSpec((tm, BRANCH_WIDTH), lambda i: (i, _GATE_OFF // BRANCH_WIDTH)),
            pl.BlockSpec((tm, MEM_WIDTH), lambda i: (i, _QM_OFF // MEM_WIDTH)),
            pl.BlockSpec((tm, d), lambda i: (i, 0)),
            pl.BlockSpec((1, 1) + kv0.shape[2:], lambda i: (0, i // nt, 0, 0)),
            pl.BlockSpec((BRANCH_WIDTH, d), lambda i: (0, 0), pipeline_mode=pl.Buffered(1)),
        ],
        out_specs=pl.BlockSpec((tm, d), lambda i: (i, 0)),
        out_shape=jax.ShapeDtypeStruct((t, d), F32),
        scratch_shapes=[pltpu.VMEM((tm, BRANCH_WIDTH), BF16)],
        compiler_params=pltpu.CompilerParams(
            dimension_semantics=("arbitrary",),
            vmem_limit_bytes=VMEM_LIMIT_BYTES),
        name="tail0",
    )(y2d, proj2d, proj2d, h2d, kv0, wo_bf16)


def _layer1_kernel(h_ref, g_ref, win_ref, wg_ref, ps_ref, kv_ref, wo_ref, fg_ref, o_ref,
                   u_s, z_s, *, tm, sub):
    t_idx = pl.program_id(1)

    @pl.when(t_idx == 0)
    def _():
        u_s[0:POOL_HALO, :] = jnp.zeros((POOL_HALO, MIX_WIDTH), F32)

    @pl.when(t_idx > 0)
    def _():
        u_s[0:POOL_HALO, :] = u_s[tm:tm + POOL_HALO, :]

    for r0 in range(0, tm, sub):
        rows = pl.ds(r0, sub)
        h = h_ref[rows, :]
        hn = _rms(h, g_ref[...]).astype(BF16)
        u0 = POOL_HALO + r0
        u_s[u0:u0 + sub, :] = jnp.dot(hn, win_ref[:, 0:MIX_WIDTH], preferred_element_type=F32)
        qm = jnp.dot(hn, win_ref[:, MIX_WIDTH:MIX_WIDTH + MEM_WIDTH],
                     preferred_element_type=F32).astype(BF16)
        gate = jnp.dot(hn, win_ref[:, MIX_WIDTH + MEM_WIDTH:], preferred_element_type=F32)

        pos = t_idx * tm + r0 + lax.broadcasted_iota(jnp.int32, (sub, 1), 0)
        for gi, win in enumerate(POOL_WINDOWS):
            lo = gi * POOL_GROUP_WIDTH
            hi = lo + POOL_GROUP_WIDTH
            w = u_s[u0 - POOL_HALO:u0 + sub, lo:hi]
            ug = w[POOL_HALO:, :]
            step = 1
            while step < win:
                w = w + pltpu.roll(w, step, 0)
                step *= 2
            wsum = w[POOL_HALO:, :]
            inv_cnt = 1.0 / jnp.minimum(pos + 1, win).astype(F32)
            pooled = (wsum * inv_cnt - ug).astype(BF16)
            mixed = jnp.dot(pooled, wg_ref[gi], preferred_element_type=F32) * ps_ref[:, lo:hi]
            z_s[rows, lo:hi] = (mixed * _silu(gate[:, lo:hi])).astype(BF16)

        _mem_attention(qm, kv_ref, z_s, gate[:, MIX_WIDTH:], rows)
        h2 = h + jnp.dot(z_s[rows, :], wo_ref[...], preferred_element_type=F32)
        o_ref[rows, :] = _rms(h2, fg_ref[...])


def _layer1(h2d, g_row, win_bf16, wg_bf16, ps_row, kv1, wo_bf16, fg_row, *, seq, tm=512, sub=256):
    t, d = h2d.shape
    b = t // seq
    nt = seq // tm
    const2 = lambda i, j: (0, 0)
    resident = dict(pipeline_mode=pl.Buffered(1))
    return pl.pallas_call(
        functools.partial(_layer1_kernel, tm=tm, sub=sub),
        grid=(b, nt),
        in_specs=[
            pl.BlockSpec((tm, d), lambda i, j: (i * nt + j, 0)),
            pl.BlockSpec((1, d), const2),
            pl.BlockSpec((d, POOL_IN_WIDTH), const2, **resident),
            pl.BlockSpec(wg_bf16.shape, lambda i, j: (0, 0, 0), **resident),
            pl.BlockSpec((1, MIX_WIDTH), const2),
            pl.BlockSpec((1, 1) + kv1.shape[2:], lambda i, j: (1, i, 0, 0)),
            pl.BlockSpec((BRANCH_WIDTH, d), const2, **resident),
            pl.BlockSpec((1, d), const2),
        ],
        out_specs=pl.BlockSpec((tm, d), lambda i, j: (i * nt + j, 0)),
        out_shape=jax.ShapeDtypeStruct((t, d), F32),
        scratch_shapes=[
            pltpu.VMEM((POOL_HALO + tm, MIX_WIDTH), F32),
            pltpu.VMEM((tm, BRANCH_WIDTH), BF16),
        ],
        compiler_params=pltpu.CompilerParams(
            dimension_semantics=("arbitrary", "arbitrary"),
            vmem_limit_bytes=VMEM_LIMIT_BYTES),
        name="layer1",
    )(h2d, g_row, win_bf16, wg_bf16, ps_row, kv1, wo_bf16, fg_row)


def kernel(x, mem, positions, ln_g, attn_w_in, attn_lambda, attn_subln_g, pool_w_in,
           pool_w_group, pool_scale, mem_norm_g, mem_w_kv, w_out, final_g):
    b, s, d = x.shape
    t = b * s
    lambda_init0 = 0.8 - 0.6 * math.exp(-0.3 * 0)

    w0 = attn_w_in[0]
    n_qk = 2 * MIX_WIDTH
    w0p = jnp.concatenate([w0[:, 3 * MIX_WIDTH + MEM_WIDTH:], w0[:, :n_qk],
                           w0[:, 3 * MIX_WIDTH:3 * MIX_WIDTH + MEM_WIDTH]], axis=1).astype(BF16)
    wvt = w0[:, n_qk:3 * MIX_WIDTH].T.astype(BF16)
    w1 = pool_w_in[0].astype(BF16)
    wg = pool_w_group[0].astype(BF16)
    wkv = mem_w_kv.astype(BF16)
    wo = w_out.astype(BF16)
    inv = ROPE_THETA ** (-jnp.arange(0, DA_QK_DIM, 2, dtype=F32) / DA_QK_DIM)
    inv_row = jnp.tile(inv, LANES // inv.shape[0]).reshape(1, LANES)
    subln_g_sq = jnp.broadcast_to(attn_subln_g[0][:, None], (DA_V_DIM, LANES))

    h2d = x.reshape(t, d)
    kv = _mem_kv(mem, mem_norm_g, wkv)

    proj, vt = _inproj0(h2d, positions.reshape(t, 1), inv_row, ln_g[0].reshape(1, d), w0p, wvt, seq=s)
    y = _diff_attn(proj.reshape(b, s, _PROJ_WIDTH), vt, attn_lambda[0:1], subln_g_sq, lambda_init0)
    h1 = _tail0(y.reshape(t, MIX_WIDTH), proj, h2d, kv, wo[0], seq=s)
    out = _layer1(h1, ln_g[1].reshape(1, d), w1, wg, pool_scale[0].reshape(1, MIX_WIDTH),
                  kv, wo[1], final_g.reshape(1, d), seq=s)
    return out.reshape(b, s, d)
```

```python
import functools
import math

import jax
import jax.numpy as jnp
from jax import lax
from jax.experimental import pallas as pl
from jax.experimental.pallas import tpu as pltpu

F32 = jnp.float32
BF16 = jnp.bfloat16

D_MODEL = 1024
MEM_HEADS = 4
MEM_HEAD_DIM = 128
MEM_WIDTH = MEM_HEADS * MEM_HEAD_DIM
BRANCH_WIDTH = 2 * D_MODEL
MIX_WIDTH = BRANCH_WIDTH - MEM_WIDTH
DA_QK_DIM = 64
DA_V_DIM = 2 * DA_QK_DIM
DA_HEADS = MIX_WIDTH // DA_V_DIM
ROPE_THETA = 10000.0
POOL_WINDOWS = (2, 4, 8, 16)
POOL_GROUP_WIDTH = MIX_WIDTH // len(POOL_WINDOWS)
POOL_HALO = 16
EPS = 1e-6
ATTN_IN_WIDTH = 3 * MIX_WIDTH + MEM_WIDTH + BRANCH_WIDTH
POOL_IN_WIDTH = MIX_WIDTH + MEM_WIDTH + BRANCH_WIDTH

LANES = 128
VMEM_LIMIT_BYTES = 56 * 1024 * 1024

_GATE_OFF = 0
_Q_OFF = BRANCH_WIDTH
_K_OFF = _Q_OFF + MIX_WIDTH
_V_OFF = _K_OFF + MIX_WIDTH
_QM_OFF = _V_OFF + MIX_WIDTH


def _rms(x, g):
    ms = jnp.mean(x * x, axis=-1, keepdims=True)
    return x * lax.rsqrt(ms + EPS) * g


def _silu(x):
    hx = 0.5 * x
    return hx + hx * jnp.tanh(hx)


def _mem_kv_kernel(mem_ref, g_ref, w_ref, o_ref):
    mem_n = _rms(mem_ref[0], g_ref[...]).astype(BF16)
    o_ref[0, 0] = jnp.dot(mem_n, w_ref[0], preferred_element_type=F32).astype(BF16)


def _mem_kv(mem, mem_norm_g, w_kv_bf16):
    depth = w_kv_bf16.shape[0]
    b, m, d = mem.shape
    return pl.pallas_call(
        _mem_kv_kernel,
        grid=(depth, b),
        in_specs=[
            pl.BlockSpec((1, m, d), lambda l, i: (i, 0, 0)),
            pl.BlockSpec((1, d), lambda l, i: (0, 0)),
            pl.BlockSpec((1, d, 2 * MEM_WIDTH), lambda l, i: (l, 0, 0)),
        ],
        out_specs=pl.BlockSpec((1, 1, m, 2 * MEM_WIDTH), lambda l, i: (l, i, 0, 0)),
        out_shape=jax.ShapeDtypeStruct((depth, b, m, 2 * MEM_WIDTH), BF16),
        compiler_params=pltpu.CompilerParams(vmem_limit_bytes=VMEM_LIMIT_BYTES),
        name="mem_kv",
    )(mem, mem_norm_g.reshape(1, d), w_kv_bf16)


def _inproj0_kernel(pos_ref, inv_ref, x_ref, g_ref, w_ref, o_ref, hn_s, cos_s, sin_s, *, tn):
    lane = lax.broadcasted_iota(jnp.int32, (1, LANES), 1)
    first_half = (lane % DA_QK_DIM) < (DA_QK_DIM // 2)
    hn_s[...] = _rms(x_ref[...], g_ref[...]).astype(BF16)
    ang = pos_ref[...].astype(F32) * inv_ref[...]
    cos_s[...] = jnp.cos(ang)
    sn = jnp.sin(ang)
    sin_s[...] = jnp.where(first_half, -sn, sn)

    qscale = DA_QK_DIM ** -0.5 * math.log2(math.e)
    for lo in range(0, w_ref.shape[1], tn):
        acc = jnp.dot(hn_s[...], w_ref[:, lo:lo + tn], preferred_element_type=F32)
        if _Q_OFF <= lo < _V_OFF:
            scale = qscale if lo < _K_OFF else 1.0
            cos = cos_s[...] * scale
            sin = sin_s[...] * scale
            for c in range(0, tn, LANES):
                xc = acc[:, c:c + LANES]
                rot = jnp.where(first_half,
                                pltpu.roll(xc, LANES - DA_QK_DIM // 2, 1),
                                pltpu.roll(xc, DA_QK_DIM // 2, 1))
                o_ref[:, lo + c:lo + c + LANES] = (xc * cos + rot * sin).astype(BF16)
        else:
            o_ref[:, lo:lo + tn] = acc.astype(BF16)


def _inproj0(h2d, pos2d, inv_row, g_row, w_bf16, *, tm=512, tn=512):
    t, d = h2d.shape
    n = w_bf16.shape[1]
    return pl.pallas_call(
        functools.partial(_inproj0_kernel, tn=tn),
        grid=(t // tm,),
        in_specs=[
            pl.BlockSpec((tm, 1), lambda i: (i, 0)),
            pl.BlockSpec((1, LANES), lambda i: (0, 0)),
            pl.BlockSpec((tm, d), lambda i: (i, 0)),
            pl.BlockSpec((1, d), lambda i: (0, 0)),
            pl.BlockSpec((d, n), lambda i: (0, 0), pipeline_mode=pl.Buffered(1)),
        ],
        out_specs=pl.BlockSpec((tm, n), lambda i: (i, 0)),
        out_shape=jax.ShapeDtypeStruct((t, n), BF16),
        scratch_shapes=[
            pltpu.VMEM((tm, d), BF16),
            pltpu.VMEM((tm, LANES), F32),
            pltpu.VMEM((tm, LANES), F32),
        ],
        compiler_params=pltpu.CompilerParams(
            dimension_semantics=("arbitrary",),
            vmem_limit_bytes=VMEM_LIMIT_BYTES),
        name="inproj0",
    )(pos2d, inv_row, h2d, g_row, w_bf16)


def _diff_attn_kernel(lam_ref, q_ref, k_ref, v_ref, qn_ref, kn_ref, g_ref, o_ref,
                      qs_s, vx_s, s_s, m_s, acc_s, *, tq, rc, nq, lambda_init):
    qi = pl.program_id(2)
    chunks = range(0, 2 * tq, rc)

    def stack_q(q):
        lane = lax.broadcasted_iota(jnp.int32, (tq, LANES), 1)
        zero = jnp.zeros_like(q)
        qs_s[0:tq, :] = jnp.where(lane < DA_QK_DIM, q, zero)
        qs_s[tq:2 * tq, :] = jnp.where(lane >= DA_QK_DIM, q, zero)

    def reset_max():
        m_s[...] = jnp.full(m_s.shape, -jnp.inf, F32)

    def diag_cols(r):
        return r % tq + rc

    def put_scores(r, k):
        ncols = k.shape[0]
        s_s[r:r + rc, 0:ncols] = lax.dot_general(qs_s[r:r + rc, :], k, (((1,), (1,)), ((), ())),
                                                 preferred_element_type=F32)

    def update(r, s, start, ncols, masked):
        vx = vx_s[pl.ds(start, ncols), :]
        cols = [s[:, c:c + LANES] for c in range(0, ncols, LANES)]
        if masked:
            row = lax.broadcasted_iota(jnp.int32, (rc, LANES), 0)
            col = lax.broadcasted_iota(jnp.int32, (rc, LANES), 1)
            for i in range(rc // LANES):
                ci = (ncols - rc) // LANES + i
                cols[ci] = jnp.where(col + i * LANES <= row, cols[ci], -jnp.inf)
        m_prev = m_s[r:r + rc, :]
        m_cur = cols[0]
        for c in cols[1:]:
            m_cur = jnp.maximum(m_cur, c)
        m_new = jnp.maximum(m_prev, jnp.max(m_cur, axis=-1, keepdims=True))
        alpha = jnp.exp2(m_prev - m_new)
        p = jnp.concatenate([jnp.exp2(c - m_new) for c in cols], axis=1).astype(BF16)
        pv = jnp.dot(p, vx, preferred_element_type=F32)
        acc_s[r:r + rc, :] = jnp.concatenate([alpha, alpha], axis=1) * acc_s[r:r + rc, :] + pv
        m_s[r:r + rc, :] = m_new

    @pl.when(qi == 0)
    def _():
        vx_s[:, 0:LANES] = v_ref[0]
        vx_s[:, LANES:2 * LANES] = jnp.ones((vx_s.shape[0], LANES), BF16)

    @pl.when((pl.program_id(0) == 0) & (pl.program_id(1) == 0) & (qi == 0))
    def _():
        reset_max()
        acc_s[...] = jnp.zeros(acc_s.shape, F32)
        stack_q(q_ref[0, 0:tq, :])
        for r in chunks:
            put_scores(r, k_ref[0, 0:diag_cols(r), :])

    def full_stage(kt, next_is_diag):
        start = pl.multiple_of(kt * tq, tq)
        for r in chunks:
            s = s_s[r:r + rc, :]
            put_scores(r, k_ref[0, pl.ds(start + tq, diag_cols(r) if next_is_diag else tq), :])
            update(r, s, start, tq, False)

    def body(kt, carry):
        full_stage(kt, False)
        return carry

    lax.fori_loop(0, qi - 1, body, 0)
    pl.when(qi > 0)(lambda: full_stage(qi - 1, True))

    def diagonal_stage(last):
        diag = pl.multiple_of(qi * tq, tq)
        stack_q(qn_ref[0] if last else q_ref[0, pl.ds(diag + tq, tq), :])
        for r in sorted(chunks, key=diag_cols, reverse=True):
            ncols = diag_cols(r)
            s = s_s[r:r + rc, 0:ncols]
            put_scores(r, kn_ref[0, 0:ncols, :] if last else k_ref[0, 0:tq, :])
            update(r, s, diag, ncols, True)
        lf = lam_ref[0]
        lam_full = (jnp.exp(jnp.sum(lf[0:1] * lf[1:2], axis=-1, keepdims=True))
                    - jnp.exp(jnp.sum(lf[2:3] * lf[3:4], axis=-1, keepdims=True))
                    + lambda_init)
        o = (acc_s[0:tq, 0:LANES] / acc_s[0:tq, LANES:2 * LANES]
             - lam_full * (acc_s[tq:2 * tq, 0:LANES] / acc_s[tq:2 * tq, LANES:2 * LANES]))
        o_ref[0] = (_rms(o, g_ref[...]) * (1.0 - lambda_init)).astype(BF16)
        reset_max()

    pl.when(qi < nq - 1)(functools.partial(diagonal_stage, False))
    pl.when(qi == nq - 1)(functools.partial(diagonal_stage, True))


def _diff_attn(proj3d, lam, subln_g_row, lambda_init, *, tq=1024, rc=256):
    b, s, _ = proj3d.shape
    qb, kb, vb = _Q_OFF // LANES, _K_OFF // LANES, _V_OFF // LANES

    def next_head(i, h):
        n = jnp.minimum(i * DA_HEADS + h + 1, b * DA_HEADS - 1)
        return n // DA_HEADS, n % DA_HEADS

    def next_tile0(col0):
        def index_map(i, h, t):
            ni, nh = next_head(i, h)
            return ni, 0, col0 + nh
        return index_map

    return pl.pallas_call(
        functools.partial(_diff_attn_kernel, tq=tq, rc=rc, nq=s // tq, lambda_init=lambda_init),
        grid=(b, DA_HEADS, s // tq),
        in_specs=[
            pl.BlockSpec((1, 4, DA_QK_DIM), lambda i, h, t: (0, 0, 0)),
            pl.BlockSpec((1, s, LANES), lambda i, h, t: (i, 0, qb + h)),
            pl.BlockSpec((1, s, LANES), lambda i, h, t: (i, 0, kb + h)),
            pl.BlockSpec((1, s, LANES), lambda i, h, t: (i, 0, vb + h)),
            pl.BlockSpec((1, tq, LANES), next_tile0(qb)),
            pl.BlockSpec((1, tq, LANES), next_tile0(kb)),
            pl.BlockSpec((1, LANES), lambda i, h, t: (0, 0)),
        ],
        out_specs=pl.BlockSpec((1, tq, LANES), lambda i, h, t: (i, t, h)),
        out_shape=jax.ShapeDtypeStruct((b, s, MIX_WIDTH), BF16),
        scratch_shapes=[
            pltpu.VMEM((2 * tq, LANES), BF16),
            pltpu.VMEM((s, 2 * LANES), BF16),
            pltpu.VMEM((2 * tq, tq), F32),
            pltpu.VMEM((2 * tq, LANES), F32),
            pltpu.VMEM((2 * tq, 2 * LANES), F32),
        ],
        compiler_params=pltpu.CompilerParams(
            dimension_semantics=("arbitrary", "arbitrary", "arbitrary"),
            vmem_limit_bytes=VMEM_LIMIT_BYTES),
        name="diff_attn",
    )(lam, proj3d, proj3d, proj3d, proj3d, proj3d, subln_g_row)


def _mem_attention(qm, kv_ref, z_s, gate_m, rows):
    for hd in range(MEM_HEADS):
        lo = hd * MEM_HEAD_DIM
        k = kv_ref[0, 0, :, lo:lo + MEM_HEAD_DIM]
        v = kv_ref[0, 0, :, MEM_WIDTH + lo:MEM_WIDTH + lo + MEM_HEAD_DIM]
        s = lax.dot_general(qm[:, lo:lo + MEM_HEAD_DIM], k, (((1,), (1,)), ((), ())),
                            preferred_element_type=F32) * (MEM_HEAD_DIM ** -0.5)
        p = jnp.exp(s - jnp.max(s, axis=-1, keepdims=True))
        l = jnp.sum(p, axis=-1, keepdims=True)
        m = jnp.dot(p.astype(BF16), v, preferred_element_type=F32) / l
        z_s[rows, MIX_WIDTH + lo:MIX_WIDTH + lo + MEM_HEAD_DIM] = (
            m * _silu(gate_m[:, lo:lo + MEM_HEAD_DIM])).astype(BF16)


def _tail0_kernel(y_ref, gate_ref, qm_ref, h_ref, kv_ref, wo_ref, o_ref, z_s, *, sub):
    for r0 in range(0, z_s.shape[0], sub):
        rows = pl.ds(r0, sub)
        gate = gate_ref[rows, :].astype(F32)
        z_s[rows, 0:MIX_WIDTH] = (y_ref[rows, :].astype(F32)
                                  * _silu(gate[:, 0:MIX_WIDTH])).astype(BF16)
        _mem_attention(qm_ref[rows, :], kv_ref, z_s, gate[:, MIX_WIDTH:], rows)
        o_ref[rows, :] = h_ref[rows, :] + jnp.dot(z_s[rows, :], wo_ref[...],
                                                  preferred_element_type=F32)


def _tail0(y2d, proj2d, h2d, kv0, wo_bf16, *, seq, tm=512, sub=256):
    t, d = h2d.shape
    nt = seq // tm
    return pl.pallas_call(
        functools.partial(_tail0_kernel, sub=sub),
        grid=(t // tm,),
        in_specs=[
            pl.BlockSpec((tm, MIX_WIDTH), lambda i: (i, 0)),
            pl.BlockSpec((tm, BRANCH_WIDTH), lambda i: (i, _GATE_OFF // BRANCH_WIDTH)),
            pl.BlockSpec((tm, MEM_WIDTH), lambda i: (i, _QM_OFF // MEM_WIDTH)),
            pl.BlockSpec((tm, d), lambda i: (i, 0)),
            pl.BlockSpec((1, 1) + kv0.shape[2:], lambda i: (0, i // nt, 0, 0)),
            pl.BlockSpec((BRANCH_WIDTH, d), lambda i: (0, 0), pipeline_mode=pl.Buffered(1)),
        ],
        out_specs=pl.BlockSpec((tm, d), lambda i: (i, 0)),
        out_shape=jax.ShapeDtypeStruct((t, d), F32),
        scratch_shapes=[pltpu.VMEM((tm, BRANCH_WIDTH), BF16)],
        compiler_params=pltpu.CompilerParams(
            dimension_semantics=("arbitrary",),
            vmem_limit_bytes=VMEM_LIMIT_BYTES),
        name="tail0",
    )(y2d, proj2d, proj2d, h2d, kv0, wo_bf16)


def _layer1_kernel(h_ref, g_ref, win_ref, wg_ref, ps_ref, kv_ref, wo_ref, fg_ref, o_ref,
                   u_s, z_s, *, tm, sub):
    t_idx = pl.program_id(1)

    @pl.when(t_idx == 0)
    def _():
        u_s[0:POOL_HALO, :] = jnp.zeros((POOL_HALO, MIX_WIDTH), F32)

    @pl.when(t_idx > 0)
    def _():
        u_s[0:POOL_HALO, :] = u_s[tm:tm + POOL_HALO, :]

    for r0 in range(0, tm, sub):
        rows = pl.ds(r0, sub)
        h = h_ref[rows, :]
        hn = _rms(h, g_ref[...]).astype(BF16)
        u0 = POOL_HALO + r0
        u_s[u0:u0 + sub, :] = jnp.dot(hn, win_ref[:, 0:MIX_WIDTH], preferred_element_type=F32)
        qm = jnp.dot(hn, win_ref[:, MIX_WIDTH:MIX_WIDTH + MEM_WIDTH],
                     preferred_element_type=F32).astype(BF16)
        gate = jnp.dot(hn, win_ref[:, MIX_WIDTH + MEM_WIDTH:], preferred_element_type=F32)

        pos = t_idx * tm + r0 + lax.broadcasted_iota(jnp.int32, (sub, 1), 0)
        for gi, win in enumerate(POOL_WINDOWS):
            lo = gi * POOL_GROUP_WIDTH
            hi = lo + POOL_GROUP_WIDTH
            w = u_s[u0 - POOL_HALO:u0 + sub, lo:hi]
            ug = w[POOL_HALO:, :]
            step = 1
            while step < win:
                w = w + pltpu.roll(w, step, 0)
                step *= 2
            wsum = w[POOL_HALO:, :]
            inv_cnt = 1.0 / jnp.minimum(pos + 1, win).astype(F32)
            pooled = (wsum * inv_cnt - ug).astype(BF16)
            mixed = jnp.dot(pooled, wg_ref[gi], preferred_element_type=F32) * ps_ref[:, lo:hi]
            z_s[rows, lo:hi] = (mixed * _silu(gate[:, lo:hi])).astype(BF16)

        _mem_attention(qm, kv_ref, z_s, gate[:, MIX_WIDTH:], rows)
        h2 = h + jnp.dot(z_s[rows, :], wo_ref[...], preferred_element_type=F32)
        o_ref[rows, :] = _rms(h2, fg_ref[...])


def _layer1(h2d, g_row, win_bf16, wg_bf16, ps_row, kv1, wo_bf16, fg_row, *, seq, tm=512, sub=256):
    t, d = h2d.shape
    b = t // seq
    nt = seq // tm
    const2 = lambda i, j: (0, 0)
    resident = dict(pipeline_mode=pl.Buffered(1))
    return pl.pallas_call(
        functools.partial(_layer1_kernel, tm=tm, sub=sub),
        grid=(b, nt),
        in_specs=[
            pl.BlockSpec((tm, d), lambda i, j: (i * nt + j, 0)),
            pl.BlockSpec((1, d), const2),
            pl.BlockSpec((d, POOL_IN_WIDTH), const2, **resident),
            pl.BlockSpec(wg_bf16.shape, lambda i, j: (0, 0, 0), **resident),
            pl.BlockSpec((1, MIX_WIDTH), const2),
            pl.BlockSpec((1, 1) + kv1.shape[2:], lambda i, j: (1, i, 0, 0)),
            pl.BlockSpec((BRANCH_WIDTH, d), const2, **resident),
            pl.BlockSpec((1, d), const2),
        ],
        out_specs=pl.BlockSpec((tm, d), lambda i, j: (i * nt + j, 0)),
        out_shape=jax.ShapeDtypeStruct((t, d), F32),
        scratch_shapes=[
            pltpu.VMEM((POOL_HALO + tm, MIX_WIDTH), F32),
            pltpu.VMEM((tm, BRANCH_WIDTH), BF16),
        ],
        compiler_params=pltpu.CompilerParams(
            dimension_semantics=("arbitrary", "arbitrary"),
            vmem_limit_bytes=VMEM_LIMIT_BYTES),
        name="layer1",
    )(h2d, g_row, win_bf16, wg_bf16, ps_row, kv1, wo_bf16, fg_row)


def kernel(x, mem, positions, ln_g, attn_w_in, attn_lambda, attn_subln_g, pool_w_in,
           pool_w_group, pool_scale, mem_norm_g, mem_w_kv, w_out, final_g):
    b, s, d = x.shape
    t = b * s
    lambda_init0 = 0.8 - 0.6 * math.exp(-0.3 * 0)

    w0 = attn_w_in[0]
    w0 = jnp.concatenate([w0[:, 3 * MIX_WIDTH + MEM_WIDTH:], w0[:, :3 * MIX_WIDTH + MEM_WIDTH]],
                         axis=1).astype(BF16)
    w1 = pool_w_in[0].astype(BF16)
    wg = pool_w_group[0].astype(BF16)
    wkv = mem_w_kv.astype(BF16)
    wo = w_out.astype(BF16)
    inv = ROPE_THETA ** (-jnp.arange(0, DA_QK_DIM, 2, dtype=F32) / DA_QK_DIM)
    inv_row = jnp.tile(inv, LANES // inv.shape[0]).reshape(1, LANES)

    h2d = x.reshape(t, d)
    kv = _mem_kv(mem, mem_norm_g, wkv)

    proj = _inproj0(h2d, positions.reshape(t, 1), inv_row, ln_g[0].reshape(1, d), w0)
    y = _diff_attn(proj.reshape(b, s, ATTN_IN_WIDTH), attn_lambda[0:1],
                   attn_subln_g[0].reshape(1, DA_V_DIM), lambda_init0)
    h1 = _tail0(y.reshape(t, MIX_WIDTH), proj, h2d, kv, wo[0], seq=s)
    out = _layer1(h1, ln_g[1].reshape(1, d), w1, wg, pool_scale[0].reshape(1, MIX_WIDTH),
                  kv, wo[1], final_g.reshape(1, d), seq=s)
    return out.reshape(b, s, d)
```

```python
import functools
import math

import jax
import jax.numpy as jnp
from jax import lax
from jax.experimental import pallas as pl
from jax.experimental.pallas import tpu as pltpu

F32 = jnp.float32
BF16 = jnp.bfloat16

D_MODEL = 1024
MEM_HEADS = 4
MEM_HEAD_DIM = 128
MEM_WIDTH = MEM_HEADS * MEM_HEAD_DIM
BRANCH_WIDTH = 2 * D_MODEL
MIX_WIDTH = BRANCH_WIDTH - MEM_WIDTH
DA_QK_DIM = 64
DA_V_DIM = 2 * DA_QK_DIM
DA_HEADS = MIX_WIDTH // DA_V_DIM
ROPE_THETA = 10000.0
POOL_WINDOWS = (2, 4, 8, 16)
POOL_GROUP_WIDTH = MIX_WIDTH // len(POOL_WINDOWS)
POOL_HALO = 16
EPS = 1e-6
ATTN_IN_WIDTH = 3 * MIX_WIDTH + MEM_WIDTH + BRANCH_WIDTH
POOL_IN_WIDTH = MIX_WIDTH + MEM_WIDTH + BRANCH_WIDTH

LANES = 128
VMEM_LIMIT_BYTES = 56 * 1024 * 1024

_Q_OFF = 0
_K_OFF = _Q_OFF + MIX_WIDTH
_V_OFF = _K_OFF + MIX_WIDTH
_QM_OFF = _V_OFF + MIX_WIDTH
_GATE_OFF = _QM_OFF + MEM_WIDTH
GATE_BLOCK = D_MODEL


def _rms(x, g):
    ms = jnp.mean(x * x, axis=-1, keepdims=True)
    return x * lax.rsqrt(ms + EPS) * g


def _silu(x):
    hx = 0.5 * x
    return hx + hx * jnp.tanh(hx)


def _mem_kv_kernel(mem_ref, g_ref, w_ref, o_ref):
    mem_n = _rms(mem_ref[0], g_ref[...]).astype(BF16)
    o_ref[0, 0] = jnp.dot(mem_n, w_ref[0], preferred_element_type=F32).astype(BF16)


def _mem_kv(mem, mem_norm_g, w_kv_bf16):
    depth = w_kv_bf16.shape[0]
    b, m, d = mem.shape
    return pl.pallas_call(
        _mem_kv_kernel,
        grid=(depth, b),
        in_specs=[
            pl.BlockSpec((1, m, d), lambda l, i: (i, 0, 0)),
            pl.BlockSpec((1, d), lambda l, i: (0, 0)),
            pl.BlockSpec((1, d, 2 * MEM_WIDTH), lambda l, i: (l, 0, 0)),
        ],
        out_specs=pl.BlockSpec((1, 1, m, 2 * MEM_WIDTH), lambda l, i: (l, i, 0, 0)),
        out_shape=jax.ShapeDtypeStruct((depth, b, m, 2 * MEM_WIDTH), BF16),
        compiler_params=pltpu.CompilerParams(vmem_limit_bytes=VMEM_LIMIT_BYTES),
        name="mem_kv",
    )(mem, mem_norm_g.reshape(1, d), w_kv_bf16)


def _inproj0_kernel(pos_ref, inv_ref, x_ref, g_ref, w_ref, o_ref, hn_s, cos_s, sin_s, *, tn):
    lane = lax.broadcasted_iota(jnp.int32, (1, LANES), 1)
    first_half = (lane % DA_QK_DIM) < (DA_QK_DIM // 2)
    hn_s[...] = _rms(x_ref[...], g_ref[...]).astype(BF16)
    ang = pos_ref[...].astype(F32) * inv_ref[...]
    cos_s[...] = jnp.cos(ang)
    sn = jnp.sin(ang)
    sin_s[...] = jnp.where(first_half, -sn, sn)

    qscale = DA_QK_DIM ** -0.5 * math.log2(math.e)
    for lo in sorted(range(0, w_ref.shape[1], tn), key=lambda c: c < _GATE_OFF):
        acc = jnp.dot(hn_s[...], w_ref[:, lo:lo + tn], preferred_element_type=F32)
        if _Q_OFF <= lo < _V_OFF:
            scale = qscale if lo < _K_OFF else 1.0
            cos = cos_s[...] * scale
            sin = sin_s[...] * scale
            for c in range(0, tn, LANES):
                xc = acc[:, c:c + LANES]
                rot = jnp.where(first_half,
                                pltpu.roll(xc, LANES - DA_QK_DIM // 2, 1),
                                pltpu.roll(xc, DA_QK_DIM // 2, 1))
                o_ref[:, lo + c:lo + c + LANES] = (xc * cos + rot * sin).astype(BF16)
        else:
            o_ref[:, lo:lo + tn] = acc.astype(BF16)


def _inproj0(h2d, pos2d, inv_row, g_row, w_bf16, *, tm=512, tn=512):
    t, d = h2d.shape
    n = w_bf16.shape[1]
    return pl.pallas_call(
        functools.partial(_inproj0_kernel, tn=tn),
        grid=(t // tm,),
        in_specs=[
            pl.BlockSpec((tm, 1), lambda i: (i, 0)),
            pl.BlockSpec((1, LANES), lambda i: (0, 0)),
            pl.BlockSpec((tm, d), lambda i: (i, 0)),
            pl.BlockSpec((1, d), lambda i: (0, 0)),
            pl.BlockSpec((d, n), lambda i: (0, 0), pipeline_mode=pl.Buffered(1)),
        ],
        out_specs=pl.BlockSpec((tm, n), lambda i: (i, 0)),
        out_shape=jax.ShapeDtypeStruct((t, n), BF16),
        scratch_shapes=[
            pltpu.VMEM((tm, d), BF16),
            pltpu.VMEM((tm, LANES), F32),
            pltpu.VMEM((tm, LANES), F32),
        ],
        compiler_params=pltpu.CompilerParams(
            dimension_semantics=("arbitrary",),
            vmem_limit_bytes=VMEM_LIMIT_BYTES),
        name="inproj0",
    )(pos2d, inv_row, h2d, g_row, w_bf16)


def _diff_attn_kernel(lam_ref, q_ref, k_ref, v_ref, qn_ref, kn_ref, g_ref, o_ref,
                      qs_s, vx_s, s_s, m_s, acc_s, *, tq, rc, nq, lambda_init):
    qi = pl.program_id(2)
    chunks = range(0, 2 * tq, rc)

    def stack_q(q):
        lane = lax.broadcasted_iota(jnp.int32, (tq, LANES), 1)
        zero = jnp.zeros_like(q)
        qs_s[0:tq, :] = jnp.where(lane < DA_QK_DIM, q, zero)
        qs_s[tq:2 * tq, :] = jnp.where(lane >= DA_QK_DIM, q, zero)

    def reset_max():
        m_s[...] = jnp.full(m_s.shape, -jnp.inf, F32)

    def diag_cols(r):
        return r % tq + rc

    def put_scores(r, k):
        ncols = k.shape[0]
        s_s[r:r + rc, 0:ncols] = lax.dot_general(qs_s[r:r + rc, :], k, (((1,), (1,)), ((), ())),
                                                 preferred_element_type=F32)

    def update(r, s, start, ncols, masked):
        vx = vx_s[pl.ds(start, ncols), :]
        cols = [s[:, c:c + LANES] for c in range(0, ncols, LANES)]
        if masked:
            row = lax.broadcasted_iota(jnp.int32, (rc, LANES), 0)
            col = lax.broadcasted_iota(jnp.int32, (rc, LANES), 1)
            for i in range(rc // LANES):
                ci = (ncols - rc) // LANES + i
                cols[ci] = jnp.where(col + i * LANES <= row, cols[ci], -jnp.inf)
        m_prev = m_s[r:r + rc, :]
        m_cur = cols[0]
        for c in cols[1:]:
            m_cur = jnp.maximum(m_cur, c)
        m_new = jnp.maximum(m_prev, jnp.max(m_cur, axis=-1, keepdims=True))
        alpha = jnp.exp2(m_prev - m_new)
        p = jnp.concatenate([jnp.exp2(c - m_new) for c in cols], axis=1).astype(BF16)
        pv = jnp.dot(p, vx, preferred_element_type=F32)
        acc_s[r:r + rc, :] = jnp.concatenate([alpha, alpha], axis=1) * acc_s[r:r + rc, :] + pv
        m_s[r:r + rc, :] = m_new

    @pl.when(qi == 0)
    def _():
        vx_s[:, 0:LANES] = v_ref[0]
        vx_s[:, LANES:2 * LANES] = jnp.ones((vx_s.shape[0], LANES), BF16)

    @pl.when((pl.program_id(0) == 0) & (pl.program_id(1) == 0) & (qi == 0))
    def _():
        reset_max()
        acc_s[...] = jnp.zeros(acc_s.shape, F32)
        stack_q(q_ref[0, 0:tq, :])
        for r in chunks:
            put_scores(r, k_ref[0, 0:diag_cols(r), :])

    def full_stage(kt, next_is_diag):
        start = pl.multiple_of(kt * tq, tq)
        for r in chunks:
            s = s_s[r:r + rc, :]
            put_scores(r, k_ref[0, pl.ds(start + tq, diag_cols(r) if next_is_diag else tq), :])
            update(r, s, start, tq, False)

    def body(kt, carry):
        full_stage(kt, False)
        return carry

    lax.fori_loop(0, qi - 1, body, 0)
    pl.when(qi > 0)(lambda: full_stage(qi - 1, True))

    def diagonal_stage(last):
        diag = pl.multiple_of(qi * tq, tq)
        stack_q(qn_ref[0] if last else q_ref[0, pl.ds(diag + tq, tq), :])
        order = sorted(chunks, key=diag_cols, reverse=True)
        late = order[-2:]
        for r in order:
            s = s_s[r:r + rc, 0:diag_cols(r)]
            if r not in late:
                put_scores(r, kn_ref[0, 0:diag_cols(r), :] if last else k_ref[0, 0:tq, :])
            update(r, s, diag, diag_cols(r), True)
        for r in late:
            put_scores(r, kn_ref[0, 0:diag_cols(r), :] if last else k_ref[0, 0:tq, :])
        lf = lam_ref[0]
        lam_full = (jnp.exp(jnp.sum(lf[0:1] * lf[1:2], axis=-1, keepdims=True))
                    - jnp.exp(jnp.sum(lf[2:3] * lf[3:4], axis=-1, keepdims=True))
                    + lambda_init)
        o = (acc_s[0:tq, 0:LANES] / acc_s[0:tq, LANES:2 * LANES]
             - lam_full * (acc_s[tq:2 * tq, 0:LANES] / acc_s[tq:2 * tq, LANES:2 * LANES]))
        o_ref[0] = (_rms(o, g_ref[...]) * (1.0 - lambda_init)).astype(BF16)
        reset_max()

    pl.when(qi < nq - 1)(functools.partial(diagonal_stage, False))
    pl.when(qi == nq - 1)(functools.partial(diagonal_stage, True))


def _diff_attn(proj3d, lam, subln_g_row, lambda_init, *, tq=1024, rc=256):
    b, s, _ = proj3d.shape
    qb, kb, vb = _Q_OFF // LANES, _K_OFF // LANES, _V_OFF // LANES

    def next_head(i, h):
        n = jnp.minimum(i * DA_HEADS + h + 1, b * DA_HEADS - 1)
        return n // DA_HEADS, n % DA_HEADS

    def next_tile0(col0):
        def index_map(i, h, t):
            ni, nh = next_head(i, h)
            return ni, 0, col0 + nh
        return index_map

    return pl.pallas_call(
        functools.partial(_diff_attn_kernel, tq=tq, rc=rc, nq=s // tq, lambda_init=lambda_init),
        grid=(b, DA_HEADS, s // tq),
        in_specs=[
            pl.BlockSpec((1, 4, DA_QK_DIM), lambda i, h, t: (0, 0, 0)),
            pl.BlockSpec((1, s, LANES), lambda i, h, t: (i, 0, qb + h)),
            pl.BlockSpec((1, s, LANES), lambda i, h, t: (i, 0, kb + h)),
            pl.BlockSpec((1, s, LANES), lambda i, h, t: (i, 0, vb + h)),
            pl.BlockSpec((1, tq, LANES), next_tile0(qb)),
            pl.BlockSpec((1, tq, LANES), next_tile0(kb)),
            pl.BlockSpec((1, LANES), lambda i, h, t: (0, 0)),
        ],
        out_specs=pl.BlockSpec((1, tq, LANES), lambda i, h, t: (i, t, h)),
        out_shape=jax.ShapeDtypeStruct((b, s, MIX_WIDTH), BF16),
        scratch_shapes=[
            pltpu.VMEM((2 * tq, LANES), BF16),
            pltpu.VMEM((s, 2 * LANES), BF16),
            pltpu.VMEM((2 * tq, tq), F32),
            pltpu.VMEM((2 * tq, LANES), F32),
            pltpu.VMEM((2 * tq, 2 * LANES), F32),
        ],
        compiler_params=pltpu.CompilerParams(
            dimension_semantics=("arbitrary", "arbitrary", "arbitrary"),
            vmem_limit_bytes=VMEM_LIMIT_BYTES),
        name="diff_attn",
    )(lam, proj3d, proj3d, proj3d, proj3d, proj3d, subln_g_row)


def _mem_attention(qm, kv_ref, z_s, gate_m, rows):
    for hd in range(MEM_HEADS):
        lo = hd * MEM_HEAD_DIM
        k = kv_ref[0, 0, :, lo:lo + MEM_HEAD_DIM]
        v = kv_ref[0, 0, :, MEM_WIDTH + lo:MEM_WIDTH + lo + MEM_HEAD_DIM]
        s = lax.dot_general(qm[:, lo:lo + MEM_HEAD_DIM], k, (((1,), (1,)), ((), ())),
                            preferred_element_type=F32) * (MEM_HEAD_DIM ** -0.5)
        p = jnp.exp(s - jnp.max(s, axis=-1, keepdims=True))
        l = jnp.sum(p, axis=-1, keepdims=True)
        m = jnp.dot(p.astype(BF16), v, preferred_element_type=F32) / l
        z_s[rows, MIX_WIDTH + lo:MIX_WIDTH + lo + MEM_HEAD_DIM] = (
            m * _silu(gate_m[:, lo:lo + MEM_HEAD_DIM])).astype(BF16)


def _tail0_kernel(y_ref, ga_ref, gb_ref, qm_ref, h_ref, kv_ref, wo_ref, o_ref, z_s, *, sub):
    n_b = MIX_WIDTH - GATE_BLOCK
    for r0 in range(0, z_s.shape[0], sub):
        rows = pl.ds(r0, sub)
        ga = ga_ref[rows, :].astype(F32)
        gb = gb_ref[rows, :].astype(F32)
        z_s[rows, 0:GATE_BLOCK] = (y_ref[rows, 0:GATE_BLOCK].astype(F32) * _silu(ga)).astype(BF16)
        z_s[rows, GATE_BLOCK:MIX_WIDTH] = (y_ref[rows, GATE_BLOCK:MIX_WIDTH].astype(F32)
                                           * _silu(gb[:, 0:n_b])).astype(BF16)
        _mem_attention(qm_ref[rows, :], kv_ref, z_s, gb[:, n_b:], rows)
        o_ref[rows, :] = h_ref[rows, :] + jnp.dot(z_s[rows, :], wo_ref[0],
                                                  preferred_element_type=F32)


def _tail0(y2d, proj2d, h2d, kv0, wo_bf16, *, seq, tm=512, sub=256):
    t, d = h2d.shape
    nt = seq // tm
    return pl.pallas_call(
        functools.partial(_tail0_kernel, sub=sub),
        grid=(t // tm,),
        in_specs=[
            pl.BlockSpec((tm, MIX_WIDTH), lambda i: (i, 0)),
            pl.BlockSpec((tm, GATE_BLOCK), lambda i: (i, _GATE_OFF // GATE_BLOCK)),
            pl.BlockSpec((tm, GATE_BLOCK), lambda i: (i, _GATE_OFF // GATE_BLOCK + 1)),
            pl.BlockSpec((tm, MEM_WIDTH), lambda i: (i, _QM_OFF // MEM_WIDTH)),
            pl.BlockSpec((tm, d), lambda i: (i, 0)),
            pl.BlockSpec((1, 1) + kv0.shape[2:], lambda i: (0, i // nt, 0, 0)),
            pl.BlockSpec((1, BRANCH_WIDTH, d), lambda i: (0, 0, 0), pipeline_mode=pl.Buffered(1)),
        ],
        out_specs=pl.BlockSpec((tm, d), lambda i: (i, 0)),
        out_shape=jax.ShapeDtypeStruct((t, d), F32),
        scratch_shapes=[pltpu.VMEM((tm, BRANCH_WIDTH), BF16)],
        compiler_params=pltpu.CompilerParams(
            dimension_semantics=("arbitrary",),
            vmem_limit_bytes=VMEM_LIMIT_BYTES),
        name="tail0",
    )(y2d, proj2d, proj2d, proj2d, h2d, kv0, wo_bf16)


def _layer1_kernel(h_ref, g_ref, win_ref, wg_ref, ps_ref, kv_ref, wo_ref, fg_ref, o_ref,
                   u_s, z_s, *, tm, sub):
    t_idx = pl.program_id(1)

    @pl.when(t_idx == 0)
    def _():
        u_s[0:POOL_HALO, :] = jnp.zeros((POOL_HALO, MIX_WIDTH), F32)

    @pl.when(t_idx > 0)
    def _():
        u_s[0:POOL_HALO, :] = u_s[tm:tm + POOL_HALO, :]

    for r0 in range(0, tm, sub):
        rows = pl.ds(r0, sub)
        h = h_ref[rows, :]
        hn = _rms(h, g_ref[...]).astype(BF16)
        u0 = POOL_HALO + r0
        u_s[u0:u0 + sub, :] = jnp.dot(hn, win_ref[:, 0:MIX_WIDTH], preferred_element_type=F32)
        qm = jnp.dot(hn, win_ref[:, MIX_WIDTH:MIX_WIDTH + MEM_WIDTH],
                     preferred_element_type=F32).astype(BF16)
        gate = jnp.dot(hn, win_ref[:, MIX_WIDTH + MEM_WIDTH:], preferred_element_type=F32)

        pos = t_idx * tm + r0 + lax.broadcasted_iota(jnp.int32, (sub, 1), 0)
        for gi, win in enumerate(POOL_WINDOWS):
            lo = gi * POOL_GROUP_WIDTH
            hi = lo + POOL_GROUP_WIDTH
            w = u_s[u0 - POOL_HALO:u0 + sub, lo:hi]
            ug = w[POOL_HALO:, :]
            step = 1
            while step < win:
                w = w + pltpu.roll(w, step, 0)
                step *= 2
            wsum = w[POOL_HALO:, :]
            inv_cnt = 1.0 / jnp.minimum(pos + 1, win).astype(F32)
            pooled = (wsum * inv_cnt - ug).astype(BF16)
            mixed = jnp.dot(pooled, wg_ref[gi], preferred_element_type=F32) * ps_ref[:, lo:hi]
            z_s[rows, lo:hi] = (mixed * _silu(gate[:, lo:hi])).astype(BF16)

        _mem_attention(qm, kv_ref, z_s, gate[:, MIX_WIDTH:], rows)
        h2 = h + jnp.dot(z_s[rows, :], wo_ref[0], preferred_element_type=F32)
        o_ref[rows, :] = _rms(h2, fg_ref[...])


def _layer1(h2d, g_row, win_bf16, wg_bf16, ps_row, kv1, wo_bf16, fg_row, *, seq, tm=512, sub=256):
    t, d = h2d.shape
    b = t // seq
    nt = seq // tm
    const2 = lambda i, j: (0, 0)
    resident = dict(pipeline_mode=pl.Buffered(1))
    return pl.pallas_call(
        functools.partial(_layer1_kernel, tm=tm, sub=sub),
        grid=(b, nt),
        in_specs=[
            pl.BlockSpec((tm, d), lambda i, j: (i * nt + j, 0)),
            pl.BlockSpec((1, d), const2),
            pl.BlockSpec((d, POOL_IN_WIDTH), const2, **resident),
            pl.BlockSpec(wg_bf16.shape, lambda i, j: (0, 0, 0), **resident),
            pl.BlockSpec((1, MIX_WIDTH), const2),
            pl.BlockSpec((1, 1) + kv1.shape[2:], lambda i, j: (1, i, 0, 0)),
            pl.BlockSpec((1, BRANCH_WIDTH, d), lambda i, j: (1, 0, 0), **resident),
            pl.BlockSpec((1, d), const2),
        ],
        out_specs=pl.BlockSpec((tm, d), lambda i, j: (i * nt + j, 0)),
        out_shape=jax.ShapeDtypeStruct((t, d), F32),
        scratch_shapes=[
            pltpu.VMEM((POOL_HALO + tm, MIX_WIDTH), F32),
            pltpu.VMEM((tm, BRANCH_WIDTH), BF16),
        ],
        compiler_params=pltpu.CompilerParams(
            dimension_semantics=("arbitrary", "arbitrary"),
            vmem_limit_bytes=VMEM_LIMIT_BYTES),
        name="layer1",
    )(h2d, g_row, win_bf16, wg_bf16, ps_row, kv1, wo_bf16, fg_row)


def kernel(x, mem, positions, ln_g, attn_w_in, attn_lambda, attn_subln_g, pool_w_in,
           pool_w_group, pool_scale, mem_norm_g, mem_w_kv, w_out, final_g):
    b, s, d = x.shape
    t = b * s
    lambda_init0 = 0.8 - 0.6 * math.exp(-0.3 * 0)

    w0 = attn_w_in[0].astype(BF16)
    w1 = pool_w_in[0].astype(BF16)
    wg = pool_w_group[0].astype(BF16)
    wkv = mem_w_kv.astype(BF16)
    wo = w_out.astype(BF16)
    inv = ROPE_THETA ** (-jnp.arange(0, DA_QK_DIM, 2, dtype=F32) / DA_QK_DIM)
    inv_row = jnp.tile(inv, LANES // inv.shape[0]).reshape(1, LANES)

    h2d = x.reshape(t, d)
    kv = _mem_kv(mem, mem_norm_g, wkv)

    proj = _inproj0(h2d, positions.reshape(t, 1), inv_row, ln_g[0].reshape(1, d), w0)
    y = _diff_attn(proj.reshape(b, s, ATTN_IN_WIDTH), attn_lambda[0:1],
                   attn_subln_g[0].reshape(1, DA_V_DIM), lambda_init0)
    h1 = _tail0(y.reshape(t, MIX_WIDTH), proj, h2d, kv, wo, seq=s)
    out = _layer1(h1, ln_g[1].reshape(1, d), w1, wg, pool_scale[0].reshape(1, MIX_WIDTH),
                  kv, wo, final_g.reshape(1, d), seq=s)
    return out.reshape(b, s, d)
```

```python
import functools
import math

import jax
import jax.numpy as jnp
from jax import lax
from jax.experimental import pallas as pl
from jax.experimental.pallas import tpu as pltpu

F32 = jnp.float32
BF16 = jnp.bfloat16

D_MODEL = 1024
MEM_HEADS = 4
MEM_HEAD_DIM = 128
MEM_WIDTH = MEM_HEADS * MEM_HEAD_DIM
BRANCH_WIDTH = 2 * D_MODEL
MIX_WIDTH = BRANCH_WIDTH - MEM_WIDTH
DA_QK_DIM = 64
DA_V_DIM = 2 * DA_QK_DIM
DA_HEADS = MIX_WIDTH // DA_V_DIM
ROPE_THETA = 10000.0
POOL_WINDOWS = (2, 4, 8, 16)
POOL_GROUP_WIDTH = MIX_WIDTH // len(POOL_WINDOWS)
POOL_HALO = 16
EPS = 1e-6
ATTN_IN_WIDTH = 3 * MIX_WIDTH + MEM_WIDTH + BRANCH_WIDTH
POOL_IN_WIDTH = MIX_WIDTH + MEM_WIDTH + BRANCH_WIDTH

LANES = 128
VMEM_LIMIT_BYTES = 56 * 1024 * 1024

_Q_OFF = 0
_K_OFF = _Q_OFF + MIX_WIDTH
_V_OFF = _K_OFF + MIX_WIDTH
_QM_OFF = _V_OFF + MIX_WIDTH
_GATE_OFF = _QM_OFF + MEM_WIDTH
GATE_BLOCK = D_MODEL


def _rms(x, g):
    ms = jnp.mean(x * x, axis=-1, keepdims=True)
    return x * lax.rsqrt(ms + EPS) * g


def _silu(x):
    hx = 0.5 * x
    return hx + hx * jnp.tanh(hx)


def _mem_kv_kernel(mem_ref, g_ref, w_ref, o_ref):
    mem_n = _rms(mem_ref[0], g_ref[...]).astype(BF16)
    o_ref[0, 0] = jnp.dot(mem_n, w_ref[0], preferred_element_type=F32).astype(BF16)


def _mem_kv(mem, mem_norm_g, w_kv_bf16):
    depth = w_kv_bf16.shape[0]
    b, m, d = mem.shape
    return pl.pallas_call(
        _mem_kv_kernel,
        grid=(depth, b),
        in_specs=[
            pl.BlockSpec((1, m, d), lambda l, i: (i, 0, 0)),
            pl.BlockSpec((1, d), lambda l, i: (0, 0)),
            pl.BlockSpec((1, d, 2 * MEM_WIDTH), lambda l, i: (l, 0, 0)),
        ],
        out_specs=pl.BlockSpec((1, 1, m, 2 * MEM_WIDTH), lambda l, i: (l, i, 0, 0)),
        out_shape=jax.ShapeDtypeStruct((depth, b, m, 2 * MEM_WIDTH), BF16),
        compiler_params=pltpu.CompilerParams(vmem_limit_bytes=VMEM_LIMIT_BYTES),
        name="mem_kv",
    )(mem, mem_norm_g.reshape(1, d), w_kv_bf16)


def _inproj0_kernel(pos_ref, inv_ref, x_ref, g_ref, w_ref, o_ref, hn_s, cos_s, sin_s, *, tn):
    lane = lax.broadcasted_iota(jnp.int32, (1, LANES), 1)
    first_half = (lane % DA_QK_DIM) < (DA_QK_DIM // 2)
    hn_s[...] = _rms(x_ref[...], g_ref[...]).astype(BF16)
    ang = pos_ref[...].astype(F32) * inv_ref[...]
    cos_s[...] = jnp.cos(ang)
    sn = jnp.sin(ang)
    sin_s[...] = jnp.where(first_half, -sn, sn)

    qscale = DA_QK_DIM ** -0.5 * math.log2(math.e)
    for lo in sorted(range(0, w_ref.shape[1], tn), key=lambda c: c < _GATE_OFF):
        acc = jnp.dot(hn_s[...], w_ref[:, lo:lo + tn], preferred_element_type=F32)
        if _Q_OFF <= lo < _V_OFF:
            scale = qscale if lo < _K_OFF else 1.0
            cos = cos_s[...] * scale
            sin = sin_s[...] * scale
            for c in range(0, tn, LANES):
                xc = acc[:, c:c + LANES]
                rot = jnp.where(first_half,
                                pltpu.roll(xc, LANES - DA_QK_DIM // 2, 1),
                                pltpu.roll(xc, DA_QK_DIM // 2, 1))
                o_ref[:, lo + c:lo + c + LANES] = (xc * cos + rot * sin).astype(BF16)
        elif lo >= _GATE_OFF:
            o_ref[:, lo:lo + tn] = _silu(acc).astype(BF16)
        else:
            o_ref[:, lo:lo + tn] = acc.astype(BF16)


def _inproj0(h2d, pos2d, inv_row, g_row, w_bf16, *, tm=512, tn=512):
    t, d = h2d.shape
    n = w_bf16.shape[1]
    return pl.pallas_call(
        functools.partial(_inproj0_kernel, tn=tn),
        grid=(t // tm,),
        in_specs=[
            pl.BlockSpec((tm, 1), lambda i: (i, 0)),
            pl.BlockSpec((1, LANES), lambda i: (0, 0)),
            pl.BlockSpec((tm, d), lambda i: (i, 0)),
            pl.BlockSpec((1, d), lambda i: (0, 0)),
            pl.BlockSpec((d, n), lambda i: (0, 0), pipeline_mode=pl.Buffered(1)),
        ],
        out_specs=pl.BlockSpec((tm, n), lambda i: (i, 0)),
        out_shape=jax.ShapeDtypeStruct((t, n), BF16),
        scratch_shapes=[
            pltpu.VMEM((tm, d), BF16),
            pltpu.VMEM((tm, LANES), F32),
            pltpu.VMEM((tm, LANES), F32),
        ],
        compiler_params=pltpu.CompilerParams(
            dimension_semantics=("arbitrary",),
            vmem_limit_bytes=VMEM_LIMIT_BYTES),
        name="inproj0",
    )(pos2d, inv_row, h2d, g_row, w_bf16)


def _diff_attn_kernel(lam_ref, q_ref, k_ref, v_ref, qn_ref, kn_ref, g_ref, o_ref,
                      qs_s, vx_s, s_s, m_s, acc_s, *, tq, rc, nq, lambda_init):
    qi = pl.program_id(2)
    chunks = range(0, 2 * tq, rc)

    def stack_q(q):
        lane = lax.broadcasted_iota(jnp.int32, (tq, LANES), 1)
        zero = jnp.zeros_like(q)
        qs_s[0:tq, :] = jnp.where(lane < DA_QK_DIM, q, zero)
        qs_s[tq:2 * tq, :] = jnp.where(lane >= DA_QK_DIM, q, zero)

    def reset_max():
        m_s[...] = jnp.full(m_s.shape, -jnp.inf, F32)

    def diag_cols(r):
        return r % tq + rc

    def put_scores(r, k):
        ncols = k.shape[0]
        s_s[r:r + rc, 0:ncols] = lax.dot_general(qs_s[r:r + rc, :], k, (((1,), (1,)), ((), ())),
                                                 preferred_element_type=F32)

    def update(r, s, start, ncols, masked):
        vx = vx_s[pl.ds(start, ncols), :]
        cols = [s[:, c:c + LANES] for c in range(0, ncols, LANES)]
        if masked:
            row = lax.broadcasted_iota(jnp.int32, (rc, LANES), 0)
            col = lax.broadcasted_iota(jnp.int32, (rc, LANES), 1)
            for i in range(rc // LANES):
                ci = (ncols - rc) // LANES + i
                cols[ci] = jnp.where(col + i * LANES <= row, cols[ci], -jnp.inf)
        m_prev = m_s[r:r + rc, :]
        m_cur = cols[0]
        for c in cols[1:]:
            m_cur = jnp.maximum(m_cur, c)
        m_new = jnp.maximum(m_prev, jnp.max(m_cur, axis=-1, keepdims=True))
        alpha = jnp.exp2(m_prev - m_new)
        p = jnp.concatenate([jnp.exp2(c - m_new) for c in cols], axis=1).astype(BF16)
        pv = jnp.dot(p, vx, preferred_element_type=F32)
        acc_s[r:r + rc, :] = jnp.concatenate([alpha, alpha], axis=1) * acc_s[r:r + rc, :] + pv
        m_s[r:r + rc, :] = m_new

    @pl.when(qi == 0)
    def _():
        vx_s[:, 0:LANES] = v_ref[0]
        vx_s[:, LANES:2 * LANES] = jnp.ones((vx_s.shape[0], LANES), BF16)

    @pl.when((pl.program_id(0) == 0) & (pl.program_id(1) == 0) & (qi == 0))
    def _():
        reset_max()
        acc_s[...] = jnp.zeros(acc_s.shape, F32)
        stack_q(q_ref[0, 0:tq, :])
        for r in chunks:
            put_scores(r, k_ref[0, 0:diag_cols(r), :])

    def full_stage(kt, next_is_diag):
        start = pl.multiple_of(kt * tq, tq)
        for r in chunks:
            s = s_s[r:r + rc, :]
            put_scores(r, k_ref[0, pl.ds(start + tq, diag_cols(r) if next_is_diag else tq), :])
            update(r, s, start, tq, False)

    def body(kt, carry):
        full_stage(kt, False)
        return carry

    lax.fori_loop(0, qi - 1, body, 0)
    pl.when(qi > 0)(lambda: full_stage(qi - 1, True))

    def diagonal_stage(last):
        diag = pl.multiple_of(qi * tq, tq)
        stack_q(qn_ref[0] if last else q_ref[0, pl.ds(diag + tq, tq), :])
        order = sorted(chunks, key=diag_cols, reverse=True)
        late = order[-2:]
        for r in order:
            s = s_s[r:r + rc, 0:diag_cols(r)]
            if r not in late:
                put_scores(r, kn_ref[0, 0:diag_cols(r), :] if last else k_ref[0, 0:tq, :])
            update(r, s, diag, diag_cols(r), True)
        for r in late:
            put_scores(r, kn_ref[0, 0:diag_cols(r), :] if last else k_ref[0, 0:tq, :])
        lf = lam_ref[0]
        lam_full = (jnp.exp(jnp.sum(lf[0:1] * lf[1:2], axis=-1, keepdims=True))
                    - jnp.exp(jnp.sum(lf[2:3] * lf[3:4], axis=-1, keepdims=True))
                    + lambda_init)
        o = (acc_s[0:tq, 0:LANES] / acc_s[0:tq, LANES:2 * LANES]
             - lam_full * (acc_s[tq:2 * tq, 0:LANES] / acc_s[tq:2 * tq, LANES:2 * LANES]))
        o_ref[0] = (_rms(o, g_ref[...]) * (1.0 - lambda_init)).astype(BF16)
        reset_max()

    pl.when(qi < nq - 1)(functools.partial(diagonal_stage, False))
    pl.when(qi == nq - 1)(functools.partial(diagonal_stage, True))


def _diff_attn(proj3d, lam, subln_g_row, lambda_init, *, tq=1024, rc=256):
    b, s, _ = proj3d.shape
    qb, kb, vb = _Q_OFF // LANES, _K_OFF // LANES, _V_OFF // LANES

    def next_head(i, h):
        n = jnp.minimum(i * DA_HEADS + h + 1, b * DA_HEADS - 1)
        return n // DA_HEADS, n % DA_HEADS

    def next_tile0(col0):
        def index_map(i, h, t):
            ni, nh = next_head(i, h)
            return ni, 0, col0 + nh
        return index_map

    return pl.pallas_call(
        functools.partial(_diff_attn_kernel, tq=tq, rc=rc, nq=s // tq, lambda_init=lambda_init),
        grid=(b, DA_HEADS, s // tq),
        in_specs=[
            pl.BlockSpec((1, 4, DA_QK_DIM), lambda i, h, t: (0, 0, 0)),
            pl.BlockSpec((1, s, LANES), lambda i, h, t: (i, 0, qb + h)),
            pl.BlockSpec((1, s, LANES), lambda i, h, t: (i, 0, kb + h)),
            pl.BlockSpec((1, s, LANES), lambda i, h, t: (i, 0, vb + h)),
            pl.BlockSpec((1, tq, LANES), next_tile0(qb)),
            pl.BlockSpec((1, tq, LANES), next_tile0(kb)),
            pl.BlockSpec((1, LANES), lambda i, h, t: (0, 0)),
        ],
        out_specs=pl.BlockSpec((1, tq, LANES), lambda i, h, t: (i, t, h)),
        out_shape=jax.ShapeDtypeStruct((b, s, MIX_WIDTH), BF16),
        scratch_shapes=[
            pltpu.VMEM((2 * tq, LANES), BF16),
            pltpu.VMEM((s, 2 * LANES), BF16),
            pltpu.VMEM((2 * tq, tq), F32),
            pltpu.VMEM((2 * tq, LANES), F32),
            pltpu.VMEM((2 * tq, 2 * LANES), F32),
        ],
        compiler_params=pltpu.CompilerParams(
            dimension_semantics=("arbitrary", "arbitrary", "arbitrary"),
            vmem_limit_bytes=VMEM_LIMIT_BYTES),
        name="diff_attn",
    )(lam, proj3d, proj3d, proj3d, proj3d, proj3d, subln_g_row)


def _mem_attention(qm, kv_ref, z_s, sgate_m, rows):
    for hd in range(MEM_HEADS):
        lo = hd * MEM_HEAD_DIM
        k = kv_ref[0, 0, :, lo:lo + MEM_HEAD_DIM]
        v = kv_ref[0, 0, :, MEM_WIDTH + lo:MEM_WIDTH + lo + MEM_HEAD_DIM]
        s = lax.dot_general(qm[:, lo:lo + MEM_HEAD_DIM], k, (((1,), (1,)), ((), ())),
                            preferred_element_type=F32) * (MEM_HEAD_DIM ** -0.5)
        p = jnp.exp(s - jnp.max(s, axis=-1, keepdims=True))
        l = jnp.sum(p, axis=-1, keepdims=True)
        m = jnp.dot(p.astype(BF16), v, preferred_element_type=F32) / l
        z_s[rows, MIX_WIDTH + lo:MIX_WIDTH + lo + MEM_HEAD_DIM] = (
            m * sgate_m[:, lo:lo + MEM_HEAD_DIM].astype(F32)).astype(BF16)


def _tail0_kernel(y_ref, sga_ref, sgb_ref, qm_ref, h_ref, kv_ref, wo_ref, o_ref, z_s, *, sub):
    n_b = MIX_WIDTH - GATE_BLOCK
    for r0 in range(0, z_s.shape[0], sub):
        rows = pl.ds(r0, sub)
        z_s[rows, 0:GATE_BLOCK] = y_ref[rows, 0:GATE_BLOCK] * sga_ref[rows, :]
        z_s[rows, GATE_BLOCK:MIX_WIDTH] = y_ref[rows, GATE_BLOCK:MIX_WIDTH] * sgb_ref[rows, 0:n_b]
        _mem_attention(qm_ref[rows, :], kv_ref, z_s, sgb_ref[rows, n_b:GATE_BLOCK], rows)
        o_ref[rows, :] = h_ref[rows, :] + jnp.dot(z_s[rows, :], wo_ref[0],
                                                  preferred_element_type=F32)


def _tail0(y2d, proj2d, h2d, kv0, wo_bf16, *, seq, tm=512, sub=256):
    t, d = h2d.shape
    nt = seq // tm
    return pl.pallas_call(
        functools.partial(_tail0_kernel, sub=sub),
        grid=(t // tm,),
        in_specs=[
            pl.BlockSpec((tm, MIX_WIDTH), lambda i: (i, 0)),
            pl.BlockSpec((tm, GATE_BLOCK), lambda i: (i, _GATE_OFF // GATE_BLOCK)),
            pl.BlockSpec((tm, GATE_BLOCK), lambda i: (i, _GATE_OFF // GATE_BLOCK + 1)),
            pl.BlockSpec((tm, MEM_WIDTH), lambda i: (i, _QM_OFF // MEM_WIDTH)),
            pl.BlockSpec((tm, d), lambda i: (i, 0)),
            pl.BlockSpec((1, 1) + kv0.shape[2:], lambda i: (0, i // nt, 0, 0)),
            pl.BlockSpec((1, BRANCH_WIDTH, d), lambda i: (0, 0, 0), pipeline_mode=pl.Buffered(1)),
        ],
        out_specs=pl.BlockSpec((tm, d), lambda i: (i, 0)),
        out_shape=jax.ShapeDtypeStruct((t, d), F32),
        scratch_shapes=[pltpu.VMEM((tm, BRANCH_WIDTH), BF16)],
        compiler_params=pltpu.CompilerParams(
            dimension_semantics=("arbitrary",),
            vmem_limit_bytes=VMEM_LIMIT_BYTES),
        name="tail0",
    )(y2d, proj2d, proj2d, proj2d, h2d, kv0, wo_bf16)


def _layer1_kernel(h_ref, g_ref, win_ref, wg_ref, ps_ref, kv_ref, wo_ref, fg_ref, o_ref,
                   u_s, z_s, *, tm, sub):
    t_idx = pl.program_id(1)

    @pl.when(t_idx == 0)
    def _():
        u_s[0:POOL_HALO, :] = jnp.zeros((POOL_HALO, MIX_WIDTH), F32)

    @pl.when(t_idx > 0)
    def _():
        u_s[0:POOL_HALO, :] = u_s[tm:tm + POOL_HALO, :]

    for r0 in range(0, tm, sub):
        rows = pl.ds(r0, sub)
        h = h_ref[rows, :]
        hn = _rms(h, g_ref[...]).astype(BF16)
        u0 = POOL_HALO + r0
        u_s[u0:u0 + sub, :] = jnp.dot(hn, win_ref[:, 0:MIX_WIDTH], preferred_element_type=F32)
        qm = jnp.dot(hn, win_ref[:, MIX_WIDTH:MIX_WIDTH + MEM_WIDTH],
                     preferred_element_type=F32).astype(BF16)
        gate = jnp.dot(hn, win_ref[:, MIX_WIDTH + MEM_WIDTH:], preferred_element_type=F32)

        pos = t_idx * tm + r0 + lax.broadcasted_iota(jnp.int32, (sub, 1), 0)
        for gi, win in enumerate(POOL_WINDOWS):
            lo = gi * POOL_GROUP_WIDTH
            hi = lo + POOL_GROUP_WIDTH
            w = u_s[u0 - POOL_HALO:u0 + sub, lo:hi]
            ug = w[POOL_HALO:, :]
            step = 1
            while step < win:
                w = w + pltpu.roll(w, step, 0)
                step *= 2
            wsum = w[POOL_HALO:, :]
            inv_cnt = 1.0 / jnp.minimum(pos + 1, win).astype(F32)
            pooled = (wsum * inv_cnt - ug).astype(BF16)
            mixed = jnp.dot(pooled, wg_ref[gi], preferred_element_type=F32) * ps_ref[:, lo:hi]
            z_s[rows, lo:hi] = (mixed * _silu(gate[:, lo:hi])).astype(BF16)

        _mem_attention(qm, kv_ref, z_s, _silu(gate[:, MIX_WIDTH:]), rows)
        h2 = h + jnp.dot(z_s[rows, :], wo_ref[0], preferred_element_type=F32)
        o_ref[rows, :] = _rms(h2, fg_ref[...])


def _layer1(h2d, g_row, win_bf16, wg_bf16, ps_row, kv1, wo_bf16, fg_row, *, seq, tm=512, sub=256):
    t, d = h2d.shape
    b = t // seq
    nt = seq // tm
    const2 = lambda i, j: (0, 0)
    resident = dict(pipeline_mode=pl.Buffered(1))
    return pl.pallas_call(
        functools.partial(_layer1_kernel, tm=tm, sub=sub),
        grid=(b, nt),
        in_specs=[
            pl.BlockSpec((tm, d), lambda i, j: (i * nt + j, 0)),
            pl.BlockSpec((1, d), const2),
            pl.BlockSpec((d, POOL_IN_WIDTH), const2, **resident),
            pl.BlockSpec(wg_bf16.shape, lambda i, j: (0, 0, 0), **resident),
            pl.BlockSpec((1, MIX_WIDTH), const2),
            pl.BlockSpec((1, 1) + kv1.shape[2:], lambda i, j: (1, i, 0, 0)),
            pl.BlockSpec((1, BRANCH_WIDTH, d), lambda i, j: (1, 0, 0), **resident),
            pl.BlockSpec((1, d), const2),
        ],
        out_specs=pl.BlockSpec((tm, d), lambda i, j: (i * nt + j, 0)),
        out_shape=jax.ShapeDtypeStruct((t, d), F32),
        scratch_shapes=[
            pltpu.VMEM((POOL_HALO + tm, MIX_WIDTH), F32),
            pltpu.VMEM((tm, BRANCH_WIDTH), BF16),
        ],
        compiler_params=pltpu.CompilerParams(
            dimension_semantics=("arbitrary", "arbitrary"),
            vmem_limit_bytes=VMEM_LIMIT_BYTES),
        name="layer1",
    )(h2d, g_row, win_bf16, wg_bf16, ps_row, kv1, wo_bf16, fg_row)


def kernel(x, mem, positions, ln_g, attn_w_in, attn_lambda, attn_subln_g, pool_w_in,
           pool_w_group, pool_scale, mem_norm_g, mem_w_kv, w_out, final_g):
    b, s, d = x.shape
    t = b * s
    lambda_init0 = 0.8 - 0.6 * math.exp(-0.3 * 0)

    w0 = attn_w_in[0].astype(BF16)
    w1 = pool_w_in[0].astype(BF16)
    wg = pool_w_group[0].astype(BF16)
    wkv = mem_w_kv.astype(BF16)
    wo = w_out.astype(BF16)
    inv = ROPE_THETA ** (-jnp.arange(0, DA_QK_DIM, 2, dtype=F32) / DA_QK_DIM)
    inv_row = jnp.tile(inv, LANES // inv.shape[0]).reshape(1, LANES)

    h2d = x.reshape(t, d)
    kv = _mem_kv(mem, mem_norm_g, wkv)

    proj = _inproj0(h2d, positions.reshape(t, 1), inv_row, ln_g[0].reshape(1, d), w0)
    y = _diff_attn(proj.reshape(b, s, ATTN_IN_WIDTH), attn_lambda[0:1],
                   attn_subln_g[0].reshape(1, DA_V_DIM), lambda_init0)
    h1 = _tail0(y.reshape(t, MIX_WIDTH), proj, h2d, kv, wo, seq=s)
    out = _layer1(h1, ln_g[1].reshape(1, d), w1, wg, pool_scale[0].reshape(1, MIX_WIDTH),
                  kv, wo, final_g.reshape(1, d), seq=s)
    return out.reshape(b, s, d)
```

```python
import functools
import math

import jax
import jax.numpy as jnp
from jax import lax
from jax.experimental import pallas as pl
from jax.experimental.pallas import tpu as pltpu

F32 = jnp.float32
BF16 = jnp.bfloat16

D_MODEL = 1024
MEM_HEADS = 4
MEM_HEAD_DIM = 128
MEM_WIDTH = MEM_HEADS * MEM_HEAD_DIM
BRANCH_WIDTH = 2 * D_MODEL
MIX_WIDTH = BRANCH_WIDTH - MEM_WIDTH
DA_QK_DIM = 64
DA_V_DIM = 2 * DA_QK_DIM
DA_HEADS = MIX_WIDTH // DA_V_DIM
ROPE_THETA = 10000.0
POOL_WINDOWS = (2, 4, 8, 16)
POOL_GROUP_WIDTH = MIX_WIDTH // len(POOL_WINDOWS)
POOL_HALO = 16
EPS = 1e-6
ATTN_IN_WIDTH = 3 * MIX_WIDTH + MEM_WIDTH + BRANCH_WIDTH
POOL_IN_WIDTH = MIX_WIDTH + MEM_WIDTH + BRANCH_WIDTH

LANES = 128
VMEM_LIMIT_BYTES = 56 * 1024 * 1024

_Q_OFF = 0
_K_OFF = _Q_OFF + MIX_WIDTH
_V_OFF = _K_OFF + MIX_WIDTH
_QM_OFF = _V_OFF + MIX_WIDTH
_GATE_OFF = _QM_OFF + MEM_WIDTH
GATE_BLOCK = D_MODEL


def _rms(x, g):
    ms = jnp.mean(x * x, axis=-1, keepdims=True)
    return x * lax.rsqrt(ms + EPS) * g


def _silu(x):
    hx = 0.5 * x
    return hx + hx * jnp.tanh(hx)


def _mem_kv_kernel(mem_ref, g_ref, w_ref, o_ref):
    mem_n = _rms(mem_ref[0], g_ref[...]).astype(BF16)
    o_ref[0, 0] = jnp.dot(mem_n, w_ref[0], preferred_element_type=F32).astype(BF16)


def _mem_kv(mem, mem_norm_g, w_kv_bf16):
    depth = w_kv_bf16.shape[0]
    b, m, d = mem.shape
    return pl.pallas_call(
        _mem_kv_kernel,
        grid=(depth, b),
        in_specs=[
            pl.BlockSpec((1, m, d), lambda l, i: (i, 0, 0)),
            pl.BlockSpec((1, d), lambda l, i: (0, 0)),
            pl.BlockSpec((1, d, 2 * MEM_WIDTH), lambda l, i: (l, 0, 0)),
        ],
        out_specs=pl.BlockSpec((1, 1, m, 2 * MEM_WIDTH), lambda l, i: (l, i, 0, 0)),
        out_shape=jax.ShapeDtypeStruct((depth, b, m, 2 * MEM_WIDTH), BF16),
        compiler_params=pltpu.CompilerParams(vmem_limit_bytes=VMEM_LIMIT_BYTES),
        name="mem_kv",
    )(mem, mem_norm_g.reshape(1, d), w_kv_bf16)


def _inproj0_kernel(pos_ref, inv_ref, x_ref, g_ref, w_ref, o_ref, hn_s, cos_s, sin_s, *, tn):
    lane = lax.broadcasted_iota(jnp.int32, (1, LANES), 1)
    first_half = (lane % DA_QK_DIM) < (DA_QK_DIM // 2)
    hn_s[...] = _rms(x_ref[...], g_ref[...]).astype(BF16)
    n_grp = LANES // (DA_QK_DIM // 2)
    grp = lane // (DA_QK_DIM // 2)
    rb = pos_ref.shape[0] // n_grp
    ang = None
    for g in range(n_grp):
        a = pos_ref[g * rb:(g + 1) * rb, :].astype(F32) * inv_ref[...]
        ang = a if ang is None else jnp.where(grp == g, a, ang)
    for packed, dst, signed in ((jnp.cos(ang), cos_s, False), (jnp.sin(ang), sin_s, True)):
        shifted = [packed] + [pltpu.roll(packed, k * (DA_QK_DIM // 2), 1) for k in range(1, n_grp)]
        for g in range(n_grp):
            tbl = shifted[0]
            for k in range(1, n_grp):
                tbl = jnp.where(grp == (g + k) % n_grp, shifted[k], tbl)
            if signed:
                tbl = jnp.where(first_half, -tbl, tbl)
            dst[g * rb:(g + 1) * rb, :] = tbl

    qscale = DA_QK_DIM ** -0.5 * math.log2(math.e)
    for lo in sorted(range(0, w_ref.shape[1], tn), key=lambda c: c < _GATE_OFF):
        acc = jnp.dot(hn_s[...], w_ref[:, lo:lo + tn], preferred_element_type=F32)
        if _Q_OFF <= lo < _V_OFF:
            scale = qscale if lo < _K_OFF else 1.0
            cos = cos_s[...] * scale
            sin = sin_s[...] * scale
            for c in range(0, tn, LANES):
                xc = acc[:, c:c + LANES]
                rot = jnp.where(first_half,
                                pltpu.roll(xc, LANES - DA_QK_DIM // 2, 1),
                                pltpu.roll(xc, DA_QK_DIM // 2, 1))
                o_ref[:, lo + c:lo + c + LANES] = (xc * cos + rot * sin).astype(BF16)
        elif lo >= _GATE_OFF:
            o_ref[:, lo:lo + tn] = _silu(acc).astype(BF16)
        else:
            o_ref[:, lo:lo + tn] = acc.astype(BF16)


def _inproj0(h2d, pos2d, inv_row, g_row, w_bf16, *, tm=512, tn=512):
    t, d = h2d.shape
    n = w_bf16.shape[1]
    return pl.pallas_call(
        functools.partial(_inproj0_kernel, tn=tn),
        grid=(t // tm,),
        in_specs=[
            pl.BlockSpec((tm, 1), lambda i: (i, 0)),
            pl.BlockSpec((1, LANES), lambda i: (0, 0)),
            pl.BlockSpec((tm, d), lambda i: (i, 0)),
            pl.BlockSpec((1, d), lambda i: (0, 0)),
            pl.BlockSpec((d, n), lambda i: (0, 0), pipeline_mode=pl.Buffered(1)),
        ],
        out_specs=pl.BlockSpec((tm, n), lambda i: (i, 0)),
        out_shape=jax.ShapeDtypeStruct((t, n), BF16),
        scratch_shapes=[
            pltpu.VMEM((tm, d), BF16),
            pltpu.VMEM((tm, LANES), F32),
            pltpu.VMEM((tm, LANES), F32),
        ],
        compiler_params=pltpu.CompilerParams(
            dimension_semantics=("arbitrary",),
            vmem_limit_bytes=VMEM_LIMIT_BYTES),
        name="inproj0",
    )(pos2d, inv_row, h2d, g_row, w_bf16)


def _diff_attn_kernel(lam_ref, q_ref, k_ref, v_ref, qn_ref, kn_ref, g_ref, o_ref,
                      qs_s, vx_s, s_s, m_s, acc_s, *, tq, rc, nq, lambda_init):
    qi = pl.program_id(2)
    chunks = range(0, 2 * tq, rc)

    def stack_q(q):
        lane = lax.broadcasted_iota(jnp.int32, (tq, LANES), 1)
        zero = jnp.zeros_like(q)
        qs_s[0:tq, :] = jnp.where(lane < DA_QK_DIM, q, zero)
        qs_s[tq:2 * tq, :] = jnp.where(lane >= DA_QK_DIM, q, zero)

    def reset_max():
        m_s[...] = jnp.full(m_s.shape, -jnp.inf, F32)

    def diag_cols(r):
        return r % tq + rc

    def put_scores(r, k):
        ncols = k.shape[0]
        s_s[r:r + rc, 0:ncols] = lax.dot_general(qs_s[r:r + rc, :], k, (((1,), (1,)), ((), ())),
                                                 preferred_element_type=F32)

    def update(r, s, start, ncols, masked):
        vx = vx_s[pl.ds(start, ncols), :]
        cols = [s[:, c:c + LANES] for c in range(0, ncols, LANES)]
        if masked:
            row = lax.broadcasted_iota(jnp.int32, (rc, LANES), 0)
            col = lax.broadcasted_iota(jnp.int32, (rc, LANES), 1)
            for i in range(rc // LANES):
                ci = (ncols - rc) // LANES + i
                cols[ci] = jnp.where(col + i * LANES <= row, cols[ci], -jnp.inf)
        m_prev = m_s[r:r + rc, :]
        m_cur = cols[0]
        for c in cols[1:]:
            m_cur = jnp.maximum(m_cur, c)
        m_new = jnp.maximum(m_prev, jnp.max(m_cur, axis=-1, keepdims=True))
        alpha = jnp.exp2(m_prev - m_new)
        p = jnp.concatenate([jnp.exp2(c - m_new) for c in cols], axis=1).astype(BF16)
        pv = jnp.dot(p, vx, preferred_element_type=F32)
        acc_s[r:r + rc, :] = jnp.concatenate([alpha, alpha], axis=1) * acc_s[r:r + rc, :] + pv
        m_s[r:r + rc, :] = m_new

    @pl.when(qi == 0)
    def _():
        vx_s[:, 0:LANES] = v_ref[0]

    @pl.when((pl.program_id(0) == 0) & (pl.program_id(1) == 0) & (qi == 0))
    def _():
        vx_s[:, LANES:2 * LANES] = jnp.ones((vx_s.shape[0], LANES), BF16)
        reset_max()
        acc_s[...] = jnp.zeros(acc_s.shape, F32)
        stack_q(q_ref[0, 0:tq, :])
        for r in chunks:
            put_scores(r, k_ref[0, 0:diag_cols(r), :])

    def full_stage(kt, next_is_diag):
        start = pl.multiple_of(kt * tq, tq)
        for r in chunks:
            s = s_s[r:r + rc, :]
            put_scores(r, k_ref[0, pl.ds(start + tq, diag_cols(r) if next_is_diag else tq), :])
            update(r, s, start, tq, False)

    def body(kt, carry):
        full_stage(kt, False)
        return carry

    lax.fori_loop(0, qi - 1, body, 0)
    pl.when(qi > 0)(lambda: full_stage(qi - 1, True))

    def diagonal_stage(last):
        diag = pl.multiple_of(qi * tq, tq)
        stack_q(qn_ref[0] if last else q_ref[0, pl.ds(diag + tq, tq), :])
        order = sorted(chunks, key=diag_cols, reverse=True)
        late = order[-2:]
        for r in order:
            s = s_s[r:r + rc, 0:diag_cols(r)]
            if r not in late:
                put_scores(r, kn_ref[0, 0:diag_cols(r), :] if last else k_ref[0, 0:tq, :])
            update(r, s, diag, diag_cols(r), True)
        for r in late:
            put_scores(r, kn_ref[0, 0:diag_cols(r), :] if last else k_ref[0, 0:tq, :])
        lf = lam_ref[0]
        lam_full = (jnp.exp(jnp.sum(lf[0:1] * lf[1:2], axis=-1, keepdims=True))
                    - jnp.exp(jnp.sum(lf[2:3] * lf[3:4], axis=-1, keepdims=True))
                    + lambda_init)
        o = (acc_s[0:tq, 0:LANES] / acc_s[0:tq, LANES:2 * LANES]
             - lam_full * (acc_s[tq:2 * tq, 0:LANES] / acc_s[tq:2 * tq, LANES:2 * LANES]))
        o_ref[0] = (_rms(o, g_ref[...]) * (1.0 - lambda_init)).astype(BF16)
        reset_max()

    pl.when(qi < nq - 1)(functools.partial(diagonal_stage, False))
    pl.when(qi == nq - 1)(functools.partial(diagonal_stage, True))


def _diff_attn(proj3d, lam, subln_g_row, lambda_init, *, tq=1024, rc=256):
    b, s, _ = proj3d.shape
    qb, kb, vb = _Q_OFF // LANES, _K_OFF // LANES, _V_OFF // LANES

    def next_head(i, h):
        n = jnp.minimum(i * DA_HEADS + h + 1, b * DA_HEADS - 1)
        return n // DA_HEADS, n % DA_HEADS

    def next_tile0(col0):
        def index_map(i, h, t):
            ni, nh = next_head(i, h)
            return ni, 0, col0 + nh
        return index_map

    return pl.pallas_call(
        functools.partial(_diff_attn_kernel, tq=tq, rc=rc, nq=s // tq, lambda_init=lambda_init),
        grid=(b, DA_HEADS, s // tq),
        in_specs=[
            pl.BlockSpec((1, 4, DA_QK_DIM), lambda i, h, t: (0, 0, 0)),
            pl.BlockSpec((1, s, LANES), lambda i, h, t: (i, 0, qb + h)),
            pl.BlockSpec((1, s, LANES), lambda i, h, t: (i, 0, kb + h)),
            pl.BlockSpec((1, s, LANES), lambda i, h, t: (i, 0, vb + h)),
            pl.BlockSpec((1, tq, LANES), next_tile0(qb)),
            pl.BlockSpec((1, tq, LANES), next_tile0(kb)),
            pl.BlockSpec((1, LANES), lambda i, h, t: (0, 0)),
        ],
        out_specs=pl.BlockSpec((1, tq, LANES), lambda i, h, t: (i, t, h)),
        out_shape=jax.ShapeDtypeStruct((b, s, MIX_WIDTH), BF16),
        scratch_shapes=[
            pltpu.VMEM((2 * tq, LANES), BF16),
            pltpu.VMEM((s, 2 * LANES), BF16),
            pltpu.VMEM((2 * tq, tq), F32),
            pltpu.VMEM((2 * tq, LANES), F32),
            pltpu.VMEM((2 * tq, 2 * LANES), F32),
        ],
        compiler_params=pltpu.CompilerParams(
            dimension_semantics=("arbitrary", "arbitrary", "arbitrary"),
            vmem_limit_bytes=VMEM_LIMIT_BYTES),
        name="diff_attn",
    )(lam, proj3d, proj3d, proj3d, proj3d, proj3d, subln_g_row)


def _mem_attention(qm, kv_ref, z_s, sgate_m, rows):
    for hd in range(MEM_HEADS):
        lo = hd * MEM_HEAD_DIM
        k = kv_ref[0, 0, :, lo:lo + MEM_HEAD_DIM]
        v = kv_ref[0, 0, :, MEM_WIDTH + lo:MEM_WIDTH + lo + MEM_HEAD_DIM]
        s = lax.dot_general(qm[:, lo:lo + MEM_HEAD_DIM], k, (((1,), (1,)), ((), ())),
                            preferred_element_type=F32) * (MEM_HEAD_DIM ** -0.5)
        p = jnp.exp(s - jnp.max(s, axis=-1, keepdims=True))
        l = jnp.sum(p, axis=-1, keepdims=True)
        m = jnp.dot(p.astype(BF16), v, preferred_element_type=F32) / l
        z_s[rows, MIX_WIDTH + lo:MIX_WIDTH + lo + MEM_HEAD_DIM] = (
            m * sgate_m[:, lo:lo + MEM_HEAD_DIM].astype(F32)).astype(BF16)


def _tail0_kernel(y_ref, sga_ref, sgb_ref, qm_ref, h_ref, kv_ref, wo_ref, o_ref, z_s, *, sub):
    n_b = MIX_WIDTH - GATE_BLOCK
    for r0 in range(0, z_s.shape[0], sub):
        rows = pl.ds(r0, sub)
        z_s[rows, 0:GATE_BLOCK] = y_ref[rows, 0:GATE_BLOCK] * sga_ref[rows, :]
        z_s[rows, GATE_BLOCK:MIX_WIDTH] = y_ref[rows, GATE_BLOCK:MIX_WIDTH] * sgb_ref[rows, 0:n_b]
        _mem_attention(qm_ref[rows, :], kv_ref, z_s, sgb_ref[rows, n_b:GATE_BLOCK], rows)
        o_ref[rows, :] = h_ref[rows, :] + jnp.dot(z_s[rows, :], wo_ref[0],
                                                  preferred_element_type=F32)


def _tail0(y2d, proj2d, h2d, kv0, wo_bf16, *, seq, tm=512, sub=256):
    t, d = h2d.shape
    nt = seq // tm
    return pl.pallas_call(
        functools.partial(_tail0_kernel, sub=sub),
        grid=(t // tm,),
        in_specs=[
            pl.BlockSpec((tm, MIX_WIDTH), lambda i: (i, 0)),
            pl.BlockSpec((tm, GATE_BLOCK), lambda i: (i, _GATE_OFF // GATE_BLOCK)),
            pl.BlockSpec((tm, GATE_BLOCK), lambda i: (i, _GATE_OFF // GATE_BLOCK + 1)),
            pl.BlockSpec((tm, MEM_WIDTH), lambda i: (i, _QM_OFF // MEM_WIDTH)),
            pl.BlockSpec((tm, d), lambda i: (i, 0)),
            pl.BlockSpec((1, 1) + kv0.shape[2:], lambda i: (0, i // nt, 0, 0)),
            pl.BlockSpec((1, BRANCH_WIDTH, d), lambda i: (0, 0, 0), pipeline_mode=pl.Buffered(1)),
        ],
        out_specs=pl.BlockSpec((tm, d), lambda i: (i, 0)),
        out_shape=jax.ShapeDtypeStruct((t, d), F32),
        scratch_shapes=[pltpu.VMEM((tm, BRANCH_WIDTH), BF16)],
        compiler_params=pltpu.CompilerParams(
            dimension_semantics=("arbitrary",),
            vmem_limit_bytes=VMEM_LIMIT_BYTES),
        name="tail0",
    )(y2d, proj2d, proj2d, proj2d, h2d, kv0, wo_bf16)


def _layer1_kernel(h_ref, g_ref, win_ref, wg_ref, ps_ref, kv_ref, wo_ref, fg_ref, o_ref,
                   u_s, z_s, *, tm, sub):
    t_idx = pl.program_id(1)

    @pl.when(t_idx == 0)
    def _():
        u_s[0:POOL_HALO, :] = jnp.zeros((POOL_HALO, MIX_WIDTH), F32)

    @pl.when(t_idx > 0)
    def _():
        u_s[0:POOL_HALO, :] = u_s[tm:tm + POOL_HALO, :]

    for r0 in range(0, tm, sub):
        rows = pl.ds(r0, sub)
        h = h_ref[rows, :]
        hn = _rms(h, g_ref[...]).astype(BF16)
        u0 = POOL_HALO + r0
        u_s[u0:u0 + sub, :] = jnp.dot(hn, win_ref[:, 0:MIX_WIDTH], preferred_element_type=F32)
        qm = jnp.dot(hn, win_ref[:, MIX_WIDTH:MIX_WIDTH + MEM_WIDTH],
                     preferred_element_type=F32).astype(BF16)
        gate = jnp.dot(hn, win_ref[:, MIX_WIDTH + MEM_WIDTH:], preferred_element_type=F32)

        pos = t_idx * tm + r0 + lax.broadcasted_iota(jnp.int32, (sub, 1), 0)
        for gi, win in enumerate(POOL_WINDOWS):
            lo = gi * POOL_GROUP_WIDTH
            hi = lo + POOL_GROUP_WIDTH
            w = u_s[u0 - POOL_HALO:u0 + sub, lo:hi]
            ug = w[POOL_HALO:, :]
            step = 1
            while step < win:
                w = w + pltpu.roll(w, step, 0)
                step *= 2
            wsum = w[POOL_HALO:, :]
            inv_cnt = 1.0 / jnp.minimum(pos + 1, win).astype(F32)
            pooled = (wsum * inv_cnt - ug).astype(BF16)
            mixed = jnp.dot(pooled, wg_ref[gi], preferred_element_type=F32) * ps_ref[:, lo:hi]
            z_s[rows, lo:hi] = (mixed * _silu(gate[:, lo:hi])).astype(BF16)

        _mem_attention(qm, kv_ref, z_s, _silu(gate[:, MIX_WIDTH:]), rows)
        h2 = h + jnp.dot(z_s[rows, :], wo_ref[0], preferred_element_type=F32)
        o_ref[rows, :] = _rms(h2, fg_ref[...])


def _layer1(h2d, g_row, win_bf16, wg_bf16, ps_row, kv1, wo_bf16, fg_row, *, seq, tm=512, sub=256):
    t, d = h2d.shape
    b = t // seq
    nt = seq // tm
    const2 = lambda i, j: (0, 0)
    resident = dict(pipeline_mode=pl.Buffered(1))
    return pl.pallas_call(
        functools.partial(_layer1_kernel, tm=tm, sub=sub),
        grid=(b, nt),
        in_specs=[
            pl.BlockSpec((tm, d), lambda i, j: (i * nt + j, 0)),
            pl.BlockSpec((1, d), const2),
            pl.BlockSpec((d, POOL_IN_WIDTH), const2, **resident),
            pl.BlockSpec(wg_bf16.shape, lambda i, j: (0, 0, 0), **resident),
            pl.BlockSpec((1, MIX_WIDTH), const2),
            pl.BlockSpec((1, 1) + kv1.shape[2:], lambda i, j: (1, i, 0, 0)),
            pl.BlockSpec((1, BRANCH_WIDTH, d), lambda i, j: (1, 0, 0), **resident),
            pl.BlockSpec((1, d), const2),
        ],
        out_specs=pl.BlockSpec((tm, d), lambda i, j: (i * nt + j, 0)),
        out_shape=jax.ShapeDtypeStruct((t, d), F32),
        scratch_shapes=[
            pltpu.VMEM((POOL_HALO + tm, MIX_WIDTH), F32),
            pltpu.VMEM((tm, BRANCH_WIDTH), BF16),
        ],
        compiler_params=pltpu.CompilerParams(
            dimension_semantics=("arbitrary", "arbitrary"),
            vmem_limit_bytes=VMEM_LIMIT_BYTES),
        name="layer1",
    )(h2d, g_row, win_bf16, wg_bf16, ps_row, kv1, wo_bf16, fg_row)


def kernel(x, mem, positions, ln_g, attn_w_in, attn_lambda, attn_subln_g, pool_w_in,
           pool_w_group, pool_scale, mem_norm_g, mem_w_kv, w_out, final_g):
    b, s, d = x.shape
    t = b * s
    lambda_init0 = 0.8 - 0.6 * math.exp(-0.3 * 0)

    w0 = attn_w_in[0].astype(BF16)
    w1 = pool_w_in[0].astype(BF16)
    wg = pool_w_group[0].astype(BF16)
    wkv = mem_w_kv.astype(BF16)
    wo = w_out.astype(BF16)
    inv = ROPE_THETA ** (-jnp.arange(0, DA_QK_DIM, 2, dtype=F32) / DA_QK_DIM)
    inv_row = jnp.tile(inv, LANES // inv.shape[0]).reshape(1, LANES)

    h2d = x.reshape(t, d)
    kv = _mem_kv(mem, mem_norm_g, wkv)

    proj = _inproj0(h2d, positions.reshape(t, 1), inv_row, ln_g[0].reshape(1, d), w0)
    y = _diff_attn(proj.reshape(b, s, ATTN_IN_WIDTH), attn_lambda[0:1],
                   attn_subln_g[0].reshape(1, DA_V_DIM), lambda_init0)
    h1 = _tail0(y.reshape(t, MIX_WIDTH), proj, h2d, kv, wo, seq=s)
    out = _layer1(h1, ln_g[1].reshape(1, d), w1, wg, pool_scale[0].reshape(1, MIX_WIDTH),
                  kv, wo, final_g.reshape(1, d), seq=s)
    return out.reshape(b, s, d)
```

```python
import functools
import math

import jax
import jax.numpy as jnp
from jax import lax
from jax.experimental import pallas as pl
from jax.experimental.pallas import tpu as pltpu

F32 = jnp.float32
BF16 = jnp.bfloat16

D_MODEL = 1024
MEM_HEADS = 4
MEM_HEAD_DIM = 128
MEM_WIDTH = MEM_HEADS * MEM_HEAD_DIM
BRANCH_WIDTH = 2 * D_MODEL
MIX_WIDTH = BRANCH_WIDTH - MEM_WIDTH
DA_QK_DIM = 64
DA_V_DIM = 2 * DA_QK_DIM
DA_HEADS = MIX_WIDTH // DA_V_DIM
ROPE_THETA = 10000.0
POOL_WINDOWS = (2, 4, 8, 16)
POOL_GROUP_WIDTH = MIX_WIDTH // len(POOL_WINDOWS)
POOL_HALO = 16
EPS = 1e-6
ATTN_IN_WIDTH = 3 * MIX_WIDTH + MEM_WIDTH + BRANCH_WIDTH
POOL_IN_WIDTH = MIX_WIDTH + MEM_WIDTH + BRANCH_WIDTH

LANES = 128
VMEM_LIMIT_BYTES = 56 * 1024 * 1024

_Q_OFF = 0
_K_OFF = _Q_OFF + MIX_WIDTH
_V_OFF = _K_OFF + MIX_WIDTH
_QM_OFF = _V_OFF + MIX_WIDTH
_GATE_OFF = _QM_OFF + MEM_WIDTH
GATE_BLOCK = D_MODEL


def _rms(x, g):
    ms = jnp.mean(x * x, axis=-1, keepdims=True)
    return x * lax.rsqrt(ms + EPS) * g


def _silu(x):
    hx = 0.5 * x
    return hx + hx * jnp.tanh(hx)


def _mem_kv_kernel(mem_ref, g_ref, w_ref, o_ref):
    mem_n = _rms(mem_ref[0], g_ref[...]).astype(BF16)
    o_ref[0, 0] = jnp.dot(mem_n, w_ref[0], preferred_element_type=F32).astype(BF16)


def _mem_kv(mem, mem_norm_g, w_kv_bf16):
    depth = w_kv_bf16.shape[0]
    b, m, d = mem.shape
    return pl.pallas_call(
        _mem_kv_kernel,
        grid=(depth, b),
        in_specs=[
            pl.BlockSpec((1, m, d), lambda l, i: (i, 0, 0)),
            pl.BlockSpec((1, d), lambda l, i: (0, 0)),
            pl.BlockSpec((1, d, 2 * MEM_WIDTH), lambda l, i: (l, 0, 0)),
        ],
        out_specs=pl.BlockSpec((1, 1, m, 2 * MEM_WIDTH), lambda l, i: (l, i, 0, 0)),
        out_shape=jax.ShapeDtypeStruct((depth, b, m, 2 * MEM_WIDTH), BF16),
        compiler_params=pltpu.CompilerParams(vmem_limit_bytes=VMEM_LIMIT_BYTES),
        name="mem_kv",
    )(mem, mem_norm_g.reshape(1, d), w_kv_bf16)


def _inproj0_kernel(pos_ref, inv_ref, x_ref, g_ref, w_ref, o_ref, hn_s, cos_s, sin_s, *, tn):
    lane = lax.broadcasted_iota(jnp.int32, (1, LANES), 1)
    first_half = (lane % DA_QK_DIM) < (DA_QK_DIM // 2)
    hn_s[...] = _rms(x_ref[...], g_ref[...]).astype(BF16)
    n_grp = LANES // (DA_QK_DIM // 2)
    grp = lane // (DA_QK_DIM // 2)
    rb = pos_ref.shape[0] // n_grp
    ang = None
    for g in range(n_grp):
        a = pos_ref[g * rb:(g + 1) * rb, :].astype(F32) * inv_ref[...]
        ang = a if ang is None else jnp.where(grp == g, a, ang)
    for packed, dst, signed in ((jnp.cos(ang), cos_s, False), (jnp.sin(ang), sin_s, True)):
        shifted = [packed] + [pltpu.roll(packed, k * (DA_QK_DIM // 2), 1) for k in range(1, n_grp)]
        for g in range(n_grp):
            tbl = shifted[0]
            for k in range(1, n_grp):
                tbl = jnp.where(grp == (g + k) % n_grp, shifted[k], tbl)
            if signed:
                tbl = jnp.where(first_half, -tbl, tbl)
            dst[g * rb:(g + 1) * rb, :] = tbl

    qscale = DA_QK_DIM ** -0.5 * math.log2(math.e)
    for lo in sorted(range(0, w_ref.shape[1], tn), key=lambda c: c < _GATE_OFF):
        acc = jnp.dot(hn_s[...], w_ref[:, lo:lo + tn], preferred_element_type=F32)
        if _Q_OFF <= lo < _V_OFF:
            scale = qscale if lo < _K_OFF else 1.0
            cos = cos_s[...] * scale
            sin = sin_s[...] * scale
            for c in range(0, tn, LANES):
                xc = acc[:, c:c + LANES]
                rot = jnp.where(first_half,
                                pltpu.roll(xc, LANES - DA_QK_DIM // 2, 1),
                                pltpu.roll(xc, DA_QK_DIM // 2, 1))
                o_ref[:, lo + c:lo + c + LANES] = (xc * cos + rot * sin).astype(BF16)
        elif lo >= _GATE_OFF:
            o_ref[:, lo:lo + tn] = _silu(acc).astype(BF16)
        else:
            o_ref[:, lo:lo + tn] = acc.astype(BF16)


def _inproj0(h2d, pos2d, inv_row, g_row, w_bf16, *, tm=512, tn=512):
    t, d = h2d.shape
    n = w_bf16.shape[1]
    return pl.pallas_call(
        functools.partial(_inproj0_kernel, tn=tn),
        grid=(t // tm,),
        in_specs=[
            pl.BlockSpec((tm, 1), lambda i: (i, 0)),
            pl.BlockSpec((1, LANES), lambda i: (0, 0)),
            pl.BlockSpec((tm, d), lambda i: (i, 0)),
            pl.BlockSpec((1, d), lambda i: (0, 0)),
            pl.BlockSpec((d, n), lambda i: (0, 0), pipeline_mode=pl.Buffered(1)),
        ],
        out_specs=pl.BlockSpec((tm, n), lambda i: (i, 0)),
        out_shape=jax.ShapeDtypeStruct((t, n), BF16),
        scratch_shapes=[
            pltpu.VMEM((tm, d), BF16),
            pltpu.VMEM((tm, LANES), F32),
            pltpu.VMEM((tm, LANES), F32),
        ],
        compiler_params=pltpu.CompilerParams(
            dimension_semantics=("arbitrary",),
            vmem_limit_bytes=VMEM_LIMIT_BYTES),
        name="inproj0",
    )(pos2d, inv_row, h2d, g_row, w_bf16)


def _diff_attn_kernel(lam_ref, q_ref, k_ref, v_ref, qn_ref, kn_ref, g_ref, o_ref,
                      qs_s, vx_s, s_s, m_s, acc_s, *, tq, rc, nq, lambda_init):
    qi = pl.program_id(2)
    chunks = range(0, 2 * tq, rc)

    def stack_q(q):
        lane = lax.broadcasted_iota(jnp.int32, (tq, LANES), 1)
        zero = jnp.zeros_like(q)
        qs_s[0:tq, :] = jnp.where(lane < DA_QK_DIM, q, zero)
        qs_s[tq:2 * tq, :] = jnp.where(lane >= DA_QK_DIM, q, zero)

    def reset_max():
        m_s[...] = jnp.full(m_s.shape, -jnp.inf, F32)

    def diag_cols(r):
        return r % tq + rc

    def put_scores(r, k):
        ncols = k.shape[0]
        s_s[r:r + rc, 0:ncols] = lax.dot_general(qs_s[r:r + rc, :], k, (((1,), (1,)), ((), ())),
                                                 preferred_element_type=F32)

    def update(r, s, start, ncols, masked):
        vx = vx_s[pl.ds(start, ncols), :]
        cols = [s[:, c:c + LANES] for c in range(0, ncols, LANES)]
        if masked:
            row = lax.broadcasted_iota(jnp.int32, (rc, LANES), 0)
            col = lax.broadcasted_iota(jnp.int32, (rc, LANES), 1)
            for i in range(rc // LANES):
                ci = (ncols - rc) // LANES + i
                cols[ci] = jnp.where(col + i * LANES <= row, cols[ci], -jnp.inf)
        m_prev = m_s[r:r + rc, :]
        m_cur = cols[0]
        for c in cols[1:]:
            m_cur = jnp.maximum(m_cur, c)
        m_new = jnp.maximum(m_prev, jnp.max(m_cur, axis=-1, keepdims=True))
        alpha = jnp.exp2(m_prev - m_new)
        p = jnp.concatenate([jnp.exp2(c - m_new) for c in cols], axis=1).astype(BF16)
        pv = jnp.dot(p, vx, preferred_element_type=F32)
        acc_s[r:r + rc, :] = jnp.concatenate([alpha, alpha], axis=1) * acc_s[r:r + rc, :] + pv
        m_s[r:r + rc, :] = m_new

    @pl.when(qi == 0)
    def _():
        vx_s[:, 0:LANES] = v_ref[0]

    @pl.when((pl.program_id(0) == 0) & (pl.program_id(1) == 0) & (qi == 0))
    def _():
        vx_s[:, LANES:2 * LANES] = jnp.ones((vx_s.shape[0], LANES), BF16)
        reset_max()
        acc_s[...] = jnp.zeros(acc_s.shape, F32)
        stack_q(q_ref[0, 0:tq, :])
        for r in chunks:
            put_scores(r, k_ref[0, 0:diag_cols(r), :])

    def full_stage(kt, next_is_diag):
        start = pl.multiple_of(kt * tq, tq)
        for r in chunks:
            s = s_s[r:r + rc, :]
            put_scores(r, k_ref[0, pl.ds(start + tq, diag_cols(r) if next_is_diag else tq), :])
            update(r, s, start, tq, False)

    def body(kt, carry):
        full_stage(kt, False)
        return carry

    lax.fori_loop(0, qi - 1, body, 0)
    pl.when(qi > 0)(lambda: full_stage(qi - 1, True))

    def diagonal_stage(last):
        diag = pl.multiple_of(qi * tq, tq)
        stack_q(qn_ref[0] if last else q_ref[0, pl.ds(diag + tq, tq), :])
        order = sorted(chunks, key=diag_cols, reverse=True)
        late = order[-2:]
        for r in order:
            s = s_s[r:r + rc, 0:diag_cols(r)]
            if r not in late:
                put_scores(r, kn_ref[0, 0:diag_cols(r), :] if last else k_ref[0, 0:tq, :])
            update(r, s, diag, diag_cols(r), True)
        for r in late:
            put_scores(r, kn_ref[0, 0:diag_cols(r), :] if last else k_ref[0, 0:tq, :])
        lf = lam_ref[0]
        lam_full = (jnp.exp(jnp.sum(lf[0:1] * lf[1:2], axis=-1, keepdims=True))
                    - jnp.exp(jnp.sum(lf[2:3] * lf[3:4], axis=-1, keepdims=True))
                    + lambda_init)
        o = (acc_s[0:tq, 0:LANES] / acc_s[0:tq, LANES:2 * LANES]
             - lam_full * (acc_s[tq:2 * tq, 0:LANES] / acc_s[tq:2 * tq, LANES:2 * LANES]))
        o_ref[0] = (_rms(o, g_ref[...]) * (1.0 - lambda_init)).astype(BF16)
        reset_max()

    pl.when(qi < nq - 1)(functools.partial(diagonal_stage, False))
    pl.when(qi == nq - 1)(functools.partial(diagonal_stage, True))


def _diff_attn(proj3d, lam, subln_g_row, lambda_init, *, tq=1024, rc=256):
    b, s, _ = proj3d.shape
    qb, kb, vb = _Q_OFF // LANES, _K_OFF // LANES, _V_OFF // LANES

    def next_head(i, h):
        n = jnp.minimum(i * DA_HEADS + h + 1, b * DA_HEADS - 1)
        return n // DA_HEADS, n % DA_HEADS

    def next_tile0(col0):
        def index_map(i, h, t):
            ni, nh = next_head(i, h)
            return ni, 0, col0 + nh
        return index_map

    return pl.pallas_call(
        functools.partial(_diff_attn_kernel, tq=tq, rc=rc, nq=s // tq, lambda_init=lambda_init),
        grid=(b, DA_HEADS, s // tq),
        in_specs=[
            pl.BlockSpec((1, 4, DA_QK_DIM), lambda i, h, t: (0, 0, 0)),
            pl.BlockSpec((1, s, LANES), lambda i, h, t: (i, 0, qb + h)),
            pl.BlockSpec((1, s, LANES), lambda i, h, t: (i, 0, kb + h)),
            pl.BlockSpec((1, s, LANES), lambda i, h, t: (i, 0, vb + h)),
            pl.BlockSpec((1, tq, LANES), next_tile0(qb)),
            pl.BlockSpec((1, tq, LANES), next_tile0(kb)),
            pl.BlockSpec((1, LANES), lambda i, h, t: (0, 0)),
        ],
        out_specs=pl.BlockSpec((1, tq, LANES), lambda i, h, t: (i, t, h)),
        out_shape=jax.ShapeDtypeStruct((b, s, MIX_WIDTH), BF16),
        scratch_shapes=[
            pltpu.VMEM((2 * tq, LANES), BF16),
            pltpu.VMEM((s, 2 * LANES), BF16),
            pltpu.VMEM((2 * tq, tq), F32),
            pltpu.VMEM((2 * tq, LANES), F32),
            pltpu.VMEM((2 * tq, 2 * LANES), F32),
        ],
        compiler_params=pltpu.CompilerParams(
            dimension_semantics=("arbitrary", "arbitrary", "arbitrary"),
            vmem_limit_bytes=VMEM_LIMIT_BYTES),
        name="diff_attn",
    )(lam, proj3d, proj3d, proj3d, proj3d, proj3d, subln_g_row)


def _mem_attention(qm, kv_ref, z_s, sgate_m, rows):
    for hd in range(MEM_HEADS):
        lo = hd * MEM_HEAD_DIM
        k = kv_ref[0, 0, :, lo:lo + MEM_HEAD_DIM]
        v = kv_ref[0, 0, :, MEM_WIDTH + lo:MEM_WIDTH + lo + MEM_HEAD_DIM]
        s = lax.dot_general(qm[:, lo:lo + MEM_HEAD_DIM], k, (((1,), (1,)), ((), ())),
                            preferred_element_type=F32) * (MEM_HEAD_DIM ** -0.5)
        p = jnp.exp(s - jnp.max(s, axis=-1, keepdims=True))
        l = jnp.sum(p, axis=-1, keepdims=True)
        m = jnp.dot(p.astype(BF16), v, preferred_element_type=F32) / l
        z_s[rows, MIX_WIDTH + lo:MIX_WIDTH + lo + MEM_HEAD_DIM] = (
            m * sgate_m[:, lo:lo + MEM_HEAD_DIM].astype(F32)).astype(BF16)


def _tail0_kernel(y_ref, sga_ref, sgb_ref, qm_ref, h_ref, kv_ref, wo_ref, o_ref, z_s, *, sub):
    n_b = MIX_WIDTH - GATE_BLOCK
    for r0 in range(0, z_s.shape[0], sub):
        rows = pl.ds(r0, sub)
        z_s[rows, 0:GATE_BLOCK] = y_ref[rows, 0:GATE_BLOCK] * sga_ref[rows, :]
        z_s[rows, GATE_BLOCK:MIX_WIDTH] = y_ref[rows, GATE_BLOCK:MIX_WIDTH] * sgb_ref[rows, 0:n_b]
        _mem_attention(qm_ref[rows, :], kv_ref, z_s, sgb_ref[rows, n_b:GATE_BLOCK], rows)
        o_ref[rows, :] = h_ref[rows, :] + jnp.dot(z_s[rows, :], wo_ref[0],
                                                  preferred_element_type=F32)


def _tail0(y2d, proj2d, h2d, kv0, wo_bf16, *, seq, tm=1024, sub=512):
    t, d = h2d.shape
    nt = seq // tm
    return pl.pallas_call(
        functools.partial(_tail0_kernel, sub=sub),
        grid=(t // tm,),
        in_specs=[
            pl.BlockSpec((tm, MIX_WIDTH), lambda i: (i, 0)),
            pl.BlockSpec((tm, GATE_BLOCK), lambda i: (i, _GATE_OFF // GATE_BLOCK)),
            pl.BlockSpec((tm, GATE_BLOCK), lambda i: (i, _GATE_OFF // GATE_BLOCK + 1)),
            pl.BlockSpec((tm, MEM_WIDTH), lambda i: (i, _QM_OFF // MEM_WIDTH)),
            pl.BlockSpec((tm, d), lambda i: (i, 0)),
            pl.BlockSpec((1, 1) + kv0.shape[2:], lambda i: (0, i // nt, 0, 0)),
            pl.BlockSpec((1, BRANCH_WIDTH, d), lambda i: (0, 0, 0), pipeline_mode=pl.Buffered(1)),
        ],
        out_specs=pl.BlockSpec((tm, d), lambda i: (i, 0)),
        out_shape=jax.ShapeDtypeStruct((t, d), F32),
        scratch_shapes=[pltpu.VMEM((tm, BRANCH_WIDTH), BF16)],
        compiler_params=pltpu.CompilerParams(
            dimension_semantics=("arbitrary",),
            vmem_limit_bytes=VMEM_LIMIT_BYTES),
        name="tail0",
    )(y2d, proj2d, proj2d, proj2d, h2d, kv0, wo_bf16)


def _layer1_kernel(h_ref, g_ref, win_ref, wg_ref, ps_ref, kv_ref, wo_ref, fg_ref, o_ref,
                   u_s, z_s, *, tm, sub):
    t_idx = pl.program_id(1)

    @pl.when(t_idx == 0)
    def _():
        u_s[0:POOL_HALO, :] = jnp.zeros((POOL_HALO, MIX_WIDTH), F32)

    @pl.when(t_idx > 0)
    def _():
        u_s[0:POOL_HALO, :] = u_s[tm:tm + POOL_HALO, :]

    for r0 in range(0, tm, sub):
        rows = pl.ds(r0, sub)
        h = h_ref[rows, :]
        hn = _rms(h, g_ref[...]).astype(BF16)
        u0 = POOL_HALO + r0
        u_s[u0:u0 + sub, :] = jnp.dot(hn, win_ref[:, 0:MIX_WIDTH], preferred_element_type=F32)
        qm = jnp.dot(hn, win_ref[:, MIX_WIDTH:MIX_WIDTH + MEM_WIDTH],
                     preferred_element_type=F32).astype(BF16)
        gate = jnp.dot(hn, win_ref[:, MIX_WIDTH + MEM_WIDTH:], preferred_element_type=F32)

        pos = t_idx * tm + r0 + lax.broadcasted_iota(jnp.int32, (sub, 1), 0)
        for gi, win in enumerate(POOL_WINDOWS):
            lo = gi * POOL_GROUP_WIDTH
            hi = lo + POOL_GROUP_WIDTH
            w = u_s[u0 - POOL_HALO:u0 + sub, lo:hi]
            ug = w[POOL_HALO:, :]
            step = 1
            while step < win:
                w = w + pltpu.roll(w, step, 0)
                step *= 2
            wsum = w[POOL_HALO:, :]
            inv_cnt = 1.0 / jnp.minimum(pos + 1, win).astype(F32)
            pooled = (wsum * inv_cnt - ug).astype(BF16)
            mixed = jnp.dot(pooled, wg_ref[gi], preferred_element_type=F32) * ps_ref[:, lo:hi]
            z_s[rows, lo:hi] = (mixed * _silu(gate[:, lo:hi])).astype(BF16)

        _mem_attention(qm, kv_ref, z_s, _silu(gate[:, MIX_WIDTH:]), rows)
        h2 = h + jnp.dot(z_s[rows, :], wo_ref[0], preferred_element_type=F32)
        o_ref[rows, :] = _rms(h2, fg_ref[...])


def _layer1(h2d, g_row, win_bf16, wg_bf16, ps_row, kv1, wo_bf16, fg_row, *, seq, tm=1024, sub=512):
    t, d = h2d.shape
    b = t // seq
    nt = seq // tm
    const2 = lambda i, j: (0, 0)
    resident = dict(pipeline_mode=pl.Buffered(1))
    return pl.pallas_call(
        functools.partial(_layer1_kernel, tm=tm, sub=sub),
        grid=(b, nt),
        in_specs=[
            pl.BlockSpec((tm, d), lambda i, j: (i * nt + j, 0)),
            pl.BlockSpec((1, d), const2),
            pl.BlockSpec((d, POOL_IN_WIDTH), const2, **resident),
            pl.BlockSpec(wg_bf16.shape, lambda i, j: (0, 0, 0), **resident),
            pl.BlockSpec((1, MIX_WIDTH), const2),
            pl.BlockSpec((1, 1) + kv1.shape[2:], lambda i, j: (1, i, 0, 0)),
            pl.BlockSpec((1, BRANCH_WIDTH, d), lambda i, j: (1, 0, 0), **resident),
            pl.BlockSpec((1, d), const2),
        ],
        out_specs=pl.BlockSpec((tm, d), lambda i, j: (i * nt + j, 0)),
        out_shape=jax.ShapeDtypeStruct((t, d), F32),
        scratch_shapes=[
            pltpu.VMEM((POOL_HALO + tm, MIX_WIDTH), F32),
            pltpu.VMEM((tm, BRANCH_WIDTH), BF16),
        ],
        compiler_params=pltpu.CompilerParams(
            dimension_semantics=("arbitrary", "arbitrary"),
            vmem_limit_bytes=VMEM_LIMIT_BYTES),
        name="layer1",
    )(h2d, g_row, win_bf16, wg_bf16, ps_row, kv1, wo_bf16, fg_row)


def kernel(x, mem, positions, ln_g, attn_w_in, attn_lambda, attn_subln_g, pool_w_in,
           pool_w_group, pool_scale, mem_norm_g, mem_w_kv, w_out, final_g):
    b, s, d = x.shape
    t = b * s
    lambda_init0 = 0.8 - 0.6 * math.exp(-0.3 * 0)

    w0 = attn_w_in[0].astype(BF16)
    w1 = pool_w_in[0].astype(BF16)
    wg = pool_w_group[0].astype(BF16)
    wkv = mem_w_kv.astype(BF16)
    wo = w_out.astype(BF16)
    inv = ROPE_THETA ** (-jnp.arange(0, DA_QK_DIM, 2, dtype=F32) / DA_QK_DIM)
    inv_row = jnp.tile(inv, LANES // inv.shape[0]).reshape(1, LANES)

    h2d = x.reshape(t, d)
    kv = _mem_kv(mem, mem_norm_g, wkv)

    proj = _inproj0(h2d, positions.reshape(t, 1), inv_row, ln_g[0].reshape(1, d), w0)
    y = _diff_attn(proj.reshape(b, s, ATTN_IN_WIDTH), attn_lambda[0:1],
                   attn_subln_g[0].reshape(1, DA_V_DIM), lambda_init0)
    h1 = _tail0(y.reshape(t, MIX_WIDTH), proj, h2d, kv, wo, seq=s)
    out = _layer1(h1, ln_g[1].reshape(1, d), w1, wg, pool_scale[0].reshape(1, MIX_WIDTH),
                  kv, wo, final_g.reshape(1, d), seq=s)
    return out.reshape(b, s, d)
```

```python
import functools
import math

import jax
import jax.numpy as jnp
from jax import lax
from jax.experimental import pallas as pl
from jax.experimental.pallas import tpu as pltpu

F32 = jnp.float32
BF16 = jnp.bfloat16

D_MODEL = 1024
MEM_HEADS = 4
MEM_HEAD_DIM = 128
MEM_WIDTH = MEM_HEADS * MEM_HEAD_DIM
BRANCH_WIDTH = 2 * D_MODEL
MIX_WIDTH = BRANCH_WIDTH - MEM_WIDTH
DA_QK_DIM = 64
DA_V_DIM = 2 * DA_QK_DIM
DA_HEADS = MIX_WIDTH // DA_V_DIM
ROPE_THETA = 10000.0
POOL_WINDOWS = (2, 4, 8, 16)
POOL_GROUP_WIDTH = MIX_WIDTH // len(POOL_WINDOWS)
POOL_HALO = 16
EPS = 1e-6
ATTN_IN_WIDTH = 3 * MIX_WIDTH + MEM_WIDTH + BRANCH_WIDTH
POOL_IN_WIDTH = MIX_WIDTH + MEM_WIDTH + BRANCH_WIDTH

LANES = 128
VMEM_LIMIT_BYTES = 56 * 1024 * 1024

_Q_OFF = 0
_K_OFF = _Q_OFF + MIX_WIDTH
_V_OFF = _K_OFF + MIX_WIDTH
_QM_OFF = _V_OFF + MIX_WIDTH
_GATE_OFF = _QM_OFF + MEM_WIDTH
GATE_BLOCK = D_MODEL


def _rms(x, g):
    ms = jnp.mean(x * x, axis=-1, keepdims=True)
    return x * lax.rsqrt(ms + EPS) * g


def _silu(x):
    hx = 0.5 * x
    return hx + hx * jnp.tanh(hx)


def _mem_kv_kernel(mem_ref, g_ref, w_ref, o_ref):
    mem_n = _rms(mem_ref[0], g_ref[...]).astype(BF16)
    o_ref[0, 0] = jnp.dot(mem_n, w_ref[0], preferred_element_type=F32).astype(BF16)


def _mem_kv(mem, mem_norm_g, w_kv_bf16):
    depth = w_kv_bf16.shape[0]
    b, m, d = mem.shape
    return pl.pallas_call(
        _mem_kv_kernel,
        grid=(depth, b),
        in_specs=[
            pl.BlockSpec((1, m, d), lambda l, i: (i, 0, 0)),
            pl.BlockSpec((1, d), lambda l, i: (0, 0)),
            pl.BlockSpec((1, d, 2 * MEM_WIDTH), lambda l, i: (l, 0, 0)),
        ],
        out_specs=pl.BlockSpec((1, 1, m, 2 * MEM_WIDTH), lambda l, i: (l, i, 0, 0)),
        out_shape=jax.ShapeDtypeStruct((depth, b, m, 2 * MEM_WIDTH), BF16),
        compiler_params=pltpu.CompilerParams(vmem_limit_bytes=VMEM_LIMIT_BYTES),
        name="mem_kv",
    )(mem, mem_norm_g.reshape(1, d), w_kv_bf16)


def _inproj0_kernel(pos_ref, inv_ref, x_ref, g_ref, w_ref, o_ref, hn_s, cos_s, sin_s, *, tn):
    lane = lax.broadcasted_iota(jnp.int32, (1, LANES), 1)
    first_half = (lane % DA_QK_DIM) < (DA_QK_DIM // 2)
    hn_s[...] = _rms(x_ref[...], g_ref[...]).astype(BF16)
    n_grp = LANES // (DA_QK_DIM // 2)
    grp = lane // (DA_QK_DIM // 2)
    rb = pos_ref.shape[0] // n_grp
    ang = None
    for g in range(n_grp):
        a = pos_ref[g * rb:(g + 1) * rb, :].astype(F32) * inv_ref[...]
        ang = a if ang is None else jnp.where(grp == g, a, ang)
    for packed, dst, signed in ((jnp.cos(ang), cos_s, False), (jnp.sin(ang), sin_s, True)):
        shifted = [packed] + [pltpu.roll(packed, k * (DA_QK_DIM // 2), 1) for k in range(1, n_grp)]
        for g in range(n_grp):
            tbl = shifted[0]
            for k in range(1, n_grp):
                tbl = jnp.where(grp == (g + k) % n_grp, shifted[k], tbl)
            if signed:
                tbl = jnp.where(first_half, -tbl, tbl)
            dst[g * rb:(g + 1) * rb, :] = tbl

    qscale = DA_QK_DIM ** -0.5 * math.log2(math.e)
    for lo in sorted(range(0, w_ref.shape[1], tn), key=lambda c: c < _GATE_OFF):
        acc = jnp.dot(hn_s[...], w_ref[:, lo:lo + tn], preferred_element_type=F32)
        if _Q_OFF <= lo < _V_OFF:
            scale = qscale if lo < _K_OFF else 1.0
            cos = cos_s[...] * scale
            sin = sin_s[...] * scale
            for c in range(0, tn, LANES):
                xc = acc[:, c:c + LANES]
                rot = jnp.where(first_half,
                                pltpu.roll(xc, LANES - DA_QK_DIM // 2, 1),
                                pltpu.roll(xc, DA_QK_DIM // 2, 1))
                o_ref[:, lo + c:lo + c + LANES] = (xc * cos + rot * sin).astype(BF16)
        elif lo >= _GATE_OFF:
            o_ref[:, lo:lo + tn] = _silu(acc).astype(BF16)
        else:
            o_ref[:, lo:lo + tn] = acc.astype(BF16)


def _inproj0(h2d, pos2d, inv_row, g_row, w_bf16, *, tm=512, tn=512):
    t, d = h2d.shape
    n = w_bf16.shape[1]
    return pl.pallas_call(
        functools.partial(_inproj0_kernel, tn=tn),
        grid=(t // tm,),
        in_specs=[
            pl.BlockSpec((tm, 1), lambda i: (i, 0)),
            pl.BlockSpec((1, LANES), lambda i: (0, 0)),
            pl.BlockSpec((tm, d), lambda i: (i, 0)),
            pl.BlockSpec((1, d), lambda i: (0, 0)),
            pl.BlockSpec((d, n), lambda i: (0, 0), pipeline_mode=pl.Buffered(1)),
        ],
        out_specs=pl.BlockSpec((tm, n), lambda i: (i, 0)),
        out_shape=jax.ShapeDtypeStruct((t, n), BF16),
        scratch_shapes=[
            pltpu.VMEM((tm, d), BF16),
            pltpu.VMEM((tm, LANES), F32),
            pltpu.VMEM((tm, LANES), F32),
        ],
        compiler_params=pltpu.CompilerParams(
            dimension_semantics=("arbitrary",),
            vmem_limit_bytes=VMEM_LIMIT_BYTES),
        name="inproj0",
    )(pos2d, inv_row, h2d, g_row, w_bf16)


def _diff_attn_kernel(lam_ref, q_ref, k_ref, v_ref, qn_ref, kn_ref, g_ref, o_ref,
                      qs_s, vx_s, s_s, m_s, acc_s, *, tq, rc, nq, nh, lambda_init):
    qi = pl.program_id(2)
    rows = 2 * tq
    chunks = [j * rows + r for r in range(0, rows, rc) for j in range(nh)]

    def head_lanes(c):
        j = c // rows
        return slice(j * LANES, (j + 1) * LANES)

    def stack_q(q):
        lane = lax.broadcasted_iota(jnp.int32, (tq, LANES), 1)
        for j in range(nh):
            qj = q[:, j * LANES:(j + 1) * LANES]
            zero = jnp.zeros_like(qj)
            qs_s[j * rows:j * rows + tq, :] = jnp.where(lane < DA_QK_DIM, qj, zero)
            qs_s[j * rows + tq:(j + 1) * rows, :] = jnp.where(lane >= DA_QK_DIM, qj, zero)

    def reset_max():
        m_s[...] = jnp.full(m_s.shape, -jnp.inf, F32)

    def diag_cols(c):
        return c % tq + rc

    def put_scores(c, k):
        ncols = k.shape[0]
        s_s[c:c + rc, 0:ncols] = lax.dot_general(qs_s[c:c + rc, :], k, (((1,), (1,)), ((), ())),
                                                 preferred_element_type=F32)

    def update(c, s, start, ncols, masked):
        vx = vx_s[c // rows, pl.ds(start, ncols), :]
        cols = [s[:, i:i + LANES] for i in range(0, ncols, LANES)]
        if masked:
            row = lax.broadcasted_iota(jnp.int32, (rc, LANES), 0)
            col = lax.broadcasted_iota(jnp.int32, (rc, LANES), 1)
            for i in range(rc // LANES):
                ci = (ncols - rc) // LANES + i
                cols[ci] = jnp.where(col + i * LANES <= row, cols[ci], -jnp.inf)
        m_prev = m_s[c:c + rc, :]
        m_cur = cols[0]
        for x in cols[1:]:
            m_cur = jnp.maximum(m_cur, x)
        m_new = jnp.maximum(m_prev, jnp.max(m_cur, axis=-1, keepdims=True))
        alpha = jnp.exp2(m_prev - m_new)
        p = jnp.concatenate([jnp.exp2(x - m_new) for x in cols], axis=1).astype(BF16)
        pv = jnp.dot(p, vx, preferred_element_type=F32)
        acc_s[c:c + rc, :] = jnp.concatenate([alpha, alpha], axis=1) * acc_s[c:c + rc, :] + pv
        m_s[c:c + rc, :] = m_new

    @pl.when(qi == 0)
    def _():
        for j in range(nh):
            vx_s[j, :, 0:LANES] = v_ref[0, :, j * LANES:(j + 1) * LANES]

    @pl.when((pl.program_id(0) == 0) & (pl.program_id(1) == 0) & (qi == 0))
    def _():
        vx_s[:, :, LANES:2 * LANES] = jnp.ones((nh, vx_s.shape[1], LANES), BF16)
        reset_max()
        acc_s[...] = jnp.zeros(acc_s.shape, F32)
        stack_q(q_ref[0, 0:tq, :])
        for c in chunks:
            put_scores(c, k_ref[0, 0:diag_cols(c), head_lanes(c)])

    def full_stage(kt, next_is_diag):
        start = pl.multiple_of(kt * tq, tq)
        for c in chunks:
            s = s_s[c:c + rc, :]
            put_scores(c, k_ref[0, pl.ds(start + tq, diag_cols(c) if next_is_diag else tq),
                                head_lanes(c)])
            update(c, s, start, tq, False)

    def body(kt, carry):
        full_stage(kt, False)
        return carry

    lax.fori_loop(0, qi - 1, body, 0)
    pl.when(qi > 0)(lambda: full_stage(qi - 1, True))

    def diagonal_stage(last):
        diag = pl.multiple_of(qi * tq, tq)
        stack_q(qn_ref[0] if last else q_ref[0, pl.ds(diag + tq, tq), :])

        def fetch(c):
            put_scores(c, kn_ref[0, 0:diag_cols(c), head_lanes(c)] if last
                       else k_ref[0, 0:tq, head_lanes(c)])

        order = sorted(chunks, key=diag_cols, reverse=True)
        late = order[-2:]
        for c in order:
            s = s_s[c:c + rc, 0:diag_cols(c)]
            if c not in late:
                fetch(c)
            update(c, s, diag, diag_cols(c), True)
        for c in late:
            fetch(c)
        lf = lam_ref[0]
        lam_full = (jnp.exp(jnp.sum(lf[0:1] * lf[1:2], axis=-1, keepdims=True))
                    - jnp.exp(jnp.sum(lf[2:3] * lf[3:4], axis=-1, keepdims=True))
                    + lambda_init)
        for j in range(nh):
            a0, a1 = j * rows, j * rows + tq
            o = (acc_s[a0:a0 + tq, 0:LANES] / acc_s[a0:a0 + tq, LANES:2 * LANES]
                 - lam_full * (acc_s[a1:a1 + tq, 0:LANES] / acc_s[a1:a1 + tq, LANES:2 * LANES]))
            o_ref[0, :, j * LANES:(j + 1) * LANES] = (
                _rms(o, g_ref[...]) * (1.0 - lambda_init)).astype(BF16)
        reset_max()

    pl.when(qi < nq - 1)(functools.partial(diagonal_stage, False))
    pl.when(qi == nq - 1)(functools.partial(diagonal_stage, True))


def _diff_attn(proj3d, lam, subln_g_row, lambda_init, *, tq=1024, rc=256, nh=2):
    b, s, _ = proj3d.shape
    width = nh * LANES
    groups = DA_HEADS // nh
    qb, kb, vb = _Q_OFF // width, _K_OFF // width, _V_OFF // width

    def next_group(i, h):
        n = jnp.minimum(i * groups + h + 1, b * groups - 1)
        return n // groups, n % groups

    def next_tile0(col0):
        def index_map(i, h, t):
            ni, nh_ = next_group(i, h)
            return ni, 0, col0 + nh_
        return index_map

    return pl.pallas_call(
        functools.partial(_diff_attn_kernel, tq=tq, rc=rc, nq=s // tq, nh=nh,
                          lambda_init=lambda_init),
        grid=(b, groups, s // tq),
        in_specs=[
            pl.BlockSpec((1, 4, DA_QK_DIM), lambda i, h, t: (0, 0, 0)),
            pl.BlockSpec((1, s, width), lambda i, h, t: (i, 0, qb + h)),
            pl.BlockSpec((1, s, width), lambda i, h, t: (i, 0, kb + h)),
            pl.BlockSpec((1, s, width), lambda i, h, t: (i, 0, vb + h)),
            pl.BlockSpec((1, tq, width), next_tile0(qb)),
            pl.BlockSpec((1, tq, width), next_tile0(kb)),
            pl.BlockSpec((1, LANES), lambda i, h, t: (0, 0)),
        ],
        out_specs=pl.BlockSpec((1, tq, width), lambda i, h, t: (i, t, h)),
        out_shape=jax.ShapeDtypeStruct((b, s, MIX_WIDTH), BF16),
        scratch_shapes=[
            pltpu.VMEM((nh * 2 * tq, LANES), BF16),
            pltpu.VMEM((nh, s, 2 * LANES), BF16),
            pltpu.VMEM((nh * 2 * tq, tq), F32),
            pltpu.VMEM((nh * 2 * tq, LANES), F32),
            pltpu.VMEM((nh * 2 * tq, 2 * LANES), F32),
        ],
        compiler_params=pltpu.CompilerParams(
            dimension_semantics=("arbitrary", "arbitrary", "arbitrary"),
            vmem_limit_bytes=VMEM_LIMIT_BYTES),
        name="diff_attn",
    )(lam, proj3d, proj3d, proj3d, proj3d, proj3d, subln_g_row)


def _mem_attention(qm, kv_ref, z_s, sgate_m, rows):
    for hd in range(MEM_HEADS):
        lo = hd * MEM_HEAD_DIM
        k = kv_ref[0, 0, :, lo:lo + MEM_HEAD_DIM]
        v = kv_ref[0, 0, :, MEM_WIDTH + lo:MEM_WIDTH + lo + MEM_HEAD_DIM]
        s = lax.dot_general(qm[:, lo:lo + MEM_HEAD_DIM], k, (((1,), (1,)), ((), ())),
                            preferred_element_type=F32) * (MEM_HEAD_DIM ** -0.5)
        p = jnp.exp(s - jnp.max(s, axis=-1, keepdims=True))
        l = jnp.sum(p, axis=-1, keepdims=True)
        m = jnp.dot(p.astype(BF16), v, preferred_element_type=F32) / l
        z_s[rows, MIX_WIDTH + lo:MIX_WIDTH + lo + MEM_HEAD_DIM] = (
            m * sgate_m[:, lo:lo + MEM_HEAD_DIM].astype(F32)).astype(BF16)


def _tail0_kernel(y_ref, sga_ref, sgb_ref, qm_ref, h_ref, kv_ref, wo_ref, o_ref, z_s, *, sub):
    n_b = MIX_WIDTH - GATE_BLOCK
    for r0 in range(0, z_s.shape[0], sub):
        rows = pl.ds(r0, sub)
        z_s[rows, 0:GATE_BLOCK] = y_ref[rows, 0:GATE_BLOCK] * sga_ref[rows, :]
        z_s[rows, GATE_BLOCK:MIX_WIDTH] = y_ref[rows, GATE_BLOCK:MIX_WIDTH] * sgb_ref[rows, 0:n_b]
        _mem_attention(qm_ref[rows, :], kv_ref, z_s, sgb_ref[rows, n_b:GATE_BLOCK], rows)
        o_ref[rows, :] = h_ref[rows, :] + jnp.dot(z_s[rows, :], wo_ref[0],
                                                  preferred_element_type=F32)


def _tail0(y2d, proj2d, h2d, kv0, wo_bf16, *, seq, tm=1024, sub=512):
    t, d = h2d.shape
    nt = seq // tm
    return pl.pallas_call(
        functools.partial(_tail0_kernel, sub=sub),
        grid=(t // tm,),
        in_specs=[
            pl.BlockSpec((tm, MIX_WIDTH), lambda i: (i, 0)),
            pl.BlockSpec((tm, GATE_BLOCK), lambda i: (i, _GATE_OFF // GATE_BLOCK)),
            pl.BlockSpec((tm, GATE_BLOCK), lambda i: (i, _GATE_OFF // GATE_BLOCK + 1)),
            pl.BlockSpec((tm, MEM_WIDTH), lambda i: (i, _QM_OFF // MEM_WIDTH)),
            pl.BlockSpec((tm, d), lambda i: (i, 0)),
            pl.BlockSpec((1, 1) + kv0.shape[2:], lambda i: (0, i // nt, 0, 0)),
            pl.BlockSpec((1, BRANCH_WIDTH, d), lambda i: (0, 0, 0), pipeline_mode=pl.Buffered(1)),
        ],
        out_specs=pl.BlockSpec((tm, d), lambda i: (i, 0)),
        out_shape=jax.ShapeDtypeStruct((t, d), F32),
        scratch_shapes=[pltpu.VMEM((tm, BRANCH_WIDTH), BF16)],
        compiler_params=pltpu.CompilerParams(
            dimension_semantics=("arbitrary",),
            vmem_limit_bytes=VMEM_LIMIT_BYTES),
        name="tail0",
    )(y2d, proj2d, proj2d, proj2d, h2d, kv0, wo_bf16)


def _layer1_kernel(h_ref, g_ref, win_ref, wg_ref, ps_ref, kv_ref, wo_ref, fg_ref, o_ref,
                   u_s, z_s, *, tm, sub):
    t_idx = pl.program_id(1)

    @pl.when(t_idx == 0)
    def _():
        u_s[0:POOL_HALO, :] = jnp.zeros((POOL_HALO, MIX_WIDTH), F32)

    @pl.when(t_idx > 0)
    def _():
        u_s[0:POOL_HALO, :] = u_s[tm:tm + POOL_HALO, :]

    for r0 in range(0, tm, sub):
        rows = pl.ds(r0, sub)
        h = h_ref[rows, :]
        hn = _rms(h, g_ref[...]).astype(BF16)
        u0 = POOL_HALO + r0
        u_s[u0:u0 + sub, :] = jnp.dot(hn, win_ref[:, 0:MIX_WIDTH], preferred_element_type=F32)
        qm = jnp.dot(hn, win_ref[:, MIX_WIDTH:MIX_WIDTH + MEM_WIDTH],
                     preferred_element_type=F32).astype(BF16)
        gate = jnp.dot(hn, win_ref[:, MIX_WIDTH + MEM_WIDTH:], preferred_element_type=F32)

        pos = t_idx * tm + r0 + lax.broadcasted_iota(jnp.int32, (sub, 1), 0)
        for gi, win in enumerate(POOL_WINDOWS):
            lo = gi * POOL_GROUP_WIDTH
            hi = lo + POOL_GROUP_WIDTH
            w = u_s[u0 - POOL_HALO:u0 + sub, lo:hi]
            ug = w[POOL_HALO:, :]
            step = 1
            while step < win:
                w = w + pltpu.roll(w, step, 0)
                step *= 2
            wsum = w[POOL_HALO:, :]
            inv_cnt = 1.0 / jnp.minimum(pos + 1, win).astype(F32)
            pooled = (wsum * inv_cnt - ug).astype(BF16)
            mixed = jnp.dot(pooled, wg_ref[gi], preferred_element_type=F32) * ps_ref[:, lo:hi]
            z_s[rows, lo:hi] = (mixed * _silu(gate[:, lo:hi])).astype(BF16)

        _mem_attention(qm, kv_ref, z_s, _silu(gate[:, MIX_WIDTH:]), rows)
        h2 = h + jnp.dot(z_s[rows, :], wo_ref[0], preferred_element_type=F32)
        o_ref[rows, :] = _rms(h2, fg_ref[...])


def _layer1(h2d, g_row, win_bf16, wg_bf16, ps_row, kv1, wo_bf16, fg_row, *, seq, tm=1024, sub=512):
    t, d = h2d.shape
    b = t // seq
    nt = seq // tm
    const2 = lambda i, j: (0, 0)
    resident = dict(pipeline_mode=pl.Buffered(1))
    return pl.pallas_call(
        functools.partial(_layer1_kernel, tm=tm, sub=sub),
        grid=(b, nt),
        in_specs=[
            pl.BlockSpec((tm, d), lambda i, j: (i * nt + j, 0)),
            pl.BlockSpec((1, d), const2),
            pl.BlockSpec((d, POOL_IN_WIDTH), const2, **resident),
            pl.BlockSpec(wg_bf16.shape, lambda i, j: (0, 0, 0), **resident),
            pl.BlockSpec((1, MIX_WIDTH), const2),
            pl.BlockSpec((1, 1) + kv1.shape[2:], lambda i, j: (1, i, 0, 0)),
            pl.BlockSpec((1, BRANCH_WIDTH, d), lambda i, j: (1, 0, 0), **resident),
            pl.BlockSpec((1, d), const2),
        ],
        out_specs=pl.BlockSpec((tm, d), lambda i, j: (i * nt + j, 0)),
        out_shape=jax.ShapeDtypeStruct((t, d), F32),
        scratch_shapes=[
            pltpu.VMEM((POOL_HALO + tm, MIX_WIDTH), F32),
            pltpu.VMEM((tm, BRANCH_WIDTH), BF16),
        ],
        compiler_params=pltpu.CompilerParams(
            dimension_semantics=("arbitrary", "arbitrary"),
            vmem_limit_bytes=VMEM_LIMIT_BYTES),
        name="layer1",
    )(h2d, g_row, win_bf16, wg_bf16, ps_row, kv1, wo_bf16, fg_row)


def kernel(x, mem, positions, ln_g, attn_w_in, attn_lambda, attn_subln_g, pool_w_in,
           pool_w_group, pool_scale, mem_norm_g, mem_w_kv, w_out, final_g):
    b, s, d = x.shape
    t = b * s
    lambda_init0 = 0.8 - 0.6 * math.exp(-0.3 * 0)

    w0 = attn_w_in[0].astype(BF16)
    w1 = pool_w_in[0].astype(BF16)
    wg = pool_w_group[0].astype(BF16)
    wkv = mem_w_kv.astype(BF16)
    wo = w_out.astype(BF16)
    inv = ROPE_THETA ** (-jnp.arange(0, DA_QK_DIM, 2, dtype=F32) / DA_QK_DIM)
    inv_row = jnp.tile(inv, LANES // inv.shape[0]).reshape(1, LANES)

    h2d = x.reshape(t, d)
    kv = _mem_kv(mem, mem_norm_g, wkv)

    proj = _inproj0(h2d, positions.reshape(t, 1), inv_row, ln_g[0].reshape(1, d), w0)
    y = _diff_attn(proj.reshape(b, s, ATTN_IN_WIDTH), attn_lambda[0:1],
                   attn_subln_g[0].reshape(1, DA_V_DIM), lambda_init0)
    h1 = _tail0(y.reshape(t, MIX_WIDTH), proj, h2d, kv, wo, seq=s)
    out = _layer1(h1, ln_g[1].reshape(1, d), w1, wg, pool_scale[0].reshape(1, MIX_WIDTH),
                  kv, wo, final_g.reshape(1, d), seq=s)
    return out.reshape(b, s, d)
```

```python
import functools
import math

import jax
import jax.numpy as jnp
from jax import lax
from jax.experimental import pallas as pl
from jax.experimental.pallas import tpu as pltpu

F32 = jnp.float32
BF16 = jnp.bfloat16

D_MODEL = 1024
MEM_HEADS = 4
MEM_HEAD_DIM = 128
MEM_WIDTH = MEM_HEADS * MEM_HEAD_DIM
BRANCH_WIDTH = 2 * D_MODEL
MIX_WIDTH = BRANCH_WIDTH - MEM_WIDTH
DA_QK_DIM = 64
DA_V_DIM = 2 * DA_QK_DIM
DA_HEADS = MIX_WIDTH // DA_V_DIM
ROPE_THETA = 10000.0
POOL_WINDOWS = (2, 4, 8, 16)
POOL_GROUP_WIDTH = MIX_WIDTH // len(POOL_WINDOWS)
POOL_HALO = 16
EPS = 1e-6
ATTN_IN_WIDTH = 3 * MIX_WIDTH + MEM_WIDTH + BRANCH_WIDTH
POOL_IN_WIDTH = MIX_WIDTH + MEM_WIDTH + BRANCH_WIDTH

LANES = 128
VMEM_LIMIT_BYTES = 56 * 1024 * 1024

_Q_OFF = 0
_K_OFF = _Q_OFF + MIX_WIDTH
_V_OFF = _K_OFF + MIX_WIDTH
_QM_OFF = _V_OFF + MIX_WIDTH
_GATE_OFF = _QM_OFF + MEM_WIDTH
GATE_BLOCK = D_MODEL


def _rms(x, g):
    ms = jnp.mean(x * x, axis=-1, keepdims=True)
    return x * lax.rsqrt(ms + EPS) * g


def _silu(x):
    hx = 0.5 * x
    return hx + hx * jnp.tanh(hx)


def _mem_kv_kernel(mem_ref, g_ref, w_ref, o_ref):
    mem_n = _rms(mem_ref[0], g_ref[...]).astype(BF16)
    o_ref[0, 0] = jnp.dot(mem_n, w_ref[0], preferred_element_type=F32).astype(BF16)


def _mem_kv(mem, mem_norm_g, w_kv_bf16):
    depth = w_kv_bf16.shape[0]
    b, m, d = mem.shape
    return pl.pallas_call(
        _mem_kv_kernel,
        grid=(depth, b),
        in_specs=[
            pl.BlockSpec((1, m, d), lambda l, i: (i, 0, 0)),
            pl.BlockSpec((1, d), lambda l, i: (0, 0)),
            pl.BlockSpec((1, d, 2 * MEM_WIDTH), lambda l, i: (l, 0, 0)),
        ],
        out_specs=pl.BlockSpec((1, 1, m, 2 * MEM_WIDTH), lambda l, i: (l, i, 0, 0)),
        out_shape=jax.ShapeDtypeStruct((depth, b, m, 2 * MEM_WIDTH), BF16),
        compiler_params=pltpu.CompilerParams(vmem_limit_bytes=VMEM_LIMIT_BYTES),
        name="mem_kv",
    )(mem, mem_norm_g.reshape(1, d), w_kv_bf16)


def _inproj0_kernel(pos_ref, inv_ref, x_ref, g_ref, w_ref, *refs, tn, n_side):
    side_in, o_ref, side_out = refs[:n_side], refs[n_side], refs[n_side + 1:2 * n_side + 1]
    hn_s, cos_s, sin_s = refs[2 * n_side + 1:]
    for src, dst in zip(side_in, side_out):
        dst[...] = src[...].astype(BF16)

    lane = lax.broadcasted_iota(jnp.int32, (1, LANES), 1)
    first_half = (lane % DA_QK_DIM) < (DA_QK_DIM // 2)
    hn_s[...] = _rms(x_ref[...], g_ref[...]).astype(BF16)
    n_grp = LANES // (DA_QK_DIM // 2)
    grp = lane // (DA_QK_DIM // 2)
    rb = pos_ref.shape[0] // n_grp
    ang = None
    for g in range(n_grp):
        a = pos_ref[g * rb:(g + 1) * rb, :].astype(F32) * inv_ref[...]
        ang = a if ang is None else jnp.where(grp == g, a, ang)
    for packed, dst, signed in ((jnp.cos(ang), cos_s, False), (jnp.sin(ang), sin_s, True)):
        shifted = [packed] + [pltpu.roll(packed, k * (DA_QK_DIM // 2), 1) for k in range(1, n_grp)]
        for g in range(n_grp):
            tbl = shifted[0]
            for k in range(1, n_grp):
                tbl = jnp.where(grp == (g + k) % n_grp, shifted[k], tbl)
            if signed:
                tbl = jnp.where(first_half, -tbl, tbl)
            dst[g * rb:(g + 1) * rb, :] = tbl

    qscale = DA_QK_DIM ** -0.5 * math.log2(math.e)
    for lo in sorted(range(0, w_ref.shape[1], tn), key=lambda c: c < _GATE_OFF):
        acc = jnp.dot(hn_s[...], w_ref[:, lo:lo + tn], preferred_element_type=F32)
        if _Q_OFF <= lo < _V_OFF:
            scale = qscale if lo < _K_OFF else 1.0
            cos = cos_s[...] * scale
            sin = sin_s[...] * scale
            for c in range(0, tn, LANES):
                xc = acc[:, c:c + LANES]
                rot = jnp.where(first_half,
                                pltpu.roll(xc, LANES - DA_QK_DIM // 2, 1),
                                pltpu.roll(xc, DA_QK_DIM // 2, 1))
                o_ref[:, lo + c:lo + c + LANES] = (xc * cos + rot * sin).astype(BF16)
        elif lo >= _GATE_OFF:
            o_ref[:, lo:lo + tn] = _silu(acc).astype(BF16)
        else:
            o_ref[:, lo:lo + tn] = acc.astype(BF16)


def _inproj0(h2d, pos2d, inv_row, g_row, w_bf16, side_f32, *, tm=512, tn=512):
    t, d = h2d.shape
    n = w_bf16.shape[1]
    steps = t // tm
    side_blocks = [(a.shape[0] // steps, a.shape[1]) for a in side_f32]
    assert all(blk[0] * steps == a.shape[0] and blk[0] % 16 == 0
               for blk, a in zip(side_blocks, side_f32))
    side_specs = [pl.BlockSpec(blk, lambda i: (i, 0)) for blk in side_blocks]
    return pl.pallas_call(
        functools.partial(_inproj0_kernel, tn=tn, n_side=len(side_f32)),
        grid=(steps,),
        in_specs=[
            pl.BlockSpec((tm, 1), lambda i: (i, 0)),
            pl.BlockSpec((1, LANES), lambda i: (0, 0)),
            pl.BlockSpec((tm, d), lambda i: (i, 0)),
            pl.BlockSpec((1, d), lambda i: (0, 0)),
            pl.BlockSpec((d, n), lambda i: (0, 0), pipeline_mode=pl.Buffered(1)),
        ] + side_specs,
        out_specs=[pl.BlockSpec((tm, n), lambda i: (i, 0))] + side_specs,
        out_shape=[jax.ShapeDtypeStruct((t, n), BF16)]
        + [jax.ShapeDtypeStruct(a.shape, BF16) for a in side_f32],
        scratch_shapes=[
            pltpu.VMEM((tm, d), BF16),
            pltpu.VMEM((tm, LANES), F32),
            pltpu.VMEM((tm, LANES), F32),
        ],
        compiler_params=pltpu.CompilerParams(
            dimension_semantics=("arbitrary",),
            vmem_limit_bytes=VMEM_LIMIT_BYTES),
        name="inproj0",
    )(pos2d, inv_row, h2d, g_row, w_bf16, *side_f32)


def _diff_attn_kernel(lam_ref, q_ref, k_ref, v_ref, qn_ref, kn_ref, g_ref, o_ref,
                      qs_s, vx_s, s_s, m_s, acc_s, *, tq, rc, nq, nh, lambda_init):
    qi = pl.program_id(2)
    rows = 2 * tq
    chunks = [j * rows + r for r in range(0, rows, rc) for j in range(nh)]

    def head_lanes(c):
        j = c // rows
        return slice(j * LANES, (j + 1) * LANES)

    def stack_q(q):
        lane = lax.broadcasted_iota(jnp.int32, (tq, LANES), 1)
        for j in range(nh):
            qj = q[:, j * LANES:(j + 1) * LANES]
            zero = jnp.zeros_like(qj)
            qs_s[j * rows:j * rows + tq, :] = jnp.where(lane < DA_QK_DIM, qj, zero)
            qs_s[j * rows + tq:(j + 1) * rows, :] = jnp.where(lane >= DA_QK_DIM, qj, zero)

    def reset_max():
        m_s[...] = jnp.full(m_s.shape, -jnp.inf, F32)

    def diag_cols(c):
        return c % tq + rc

    def put_scores(c, k):
        ncols = k.shape[0]
        s_s[c:c + rc, 0:ncols] = lax.dot_general(qs_s[c:c + rc, :], k, (((1,), (1,)), ((), ())),
                                                 preferred_element_type=F32)

    def update(c, s, start, ncols, masked):
        vx = vx_s[c // rows, pl.ds(start, ncols), :]
        cols = [s[:, i:i + LANES] for i in range(0, ncols, LANES)]
        if masked:
            row = lax.broadcasted_iota(jnp.int32, (rc, LANES), 0)
            col = lax.broadcasted_iota(jnp.int32, (rc, LANES), 1)
            for i in range(rc // LANES):
                ci = (ncols - rc) // LANES + i
                cols[ci] = jnp.where(col + i * LANES <= row, cols[ci], -jnp.inf)
        m_prev = m_s[c:c + rc, :]
        m_cur = cols[0]
        for x in cols[1:]:
            m_cur = jnp.maximum(m_cur, x)
        m_new = jnp.maximum(m_prev, jnp.max(m_cur, axis=-1, keepdims=True))
        alpha = jnp.exp2(m_prev - m_new)
        p = jnp.concatenate([jnp.exp2(x - m_new) for x in cols], axis=1).astype(BF16)
        pv = jnp.dot(p, vx, preferred_element_type=F32)
        acc_s[c:c + rc, :] = jnp.concatenate([alpha, alpha], axis=1) * acc_s[c:c + rc, :] + pv
        m_s[c:c + rc, :] = m_new

    @pl.when(qi == 0)
    def _():
        for j in range(nh):
            vx_s[j, :, 0:LANES] = v_ref[0, :, j * LANES:(j + 1) * LANES]

    @pl.when((pl.program_id(0) == 0) & (pl.program_id(1) == 0) & (qi == 0))
    def _():
        vx_s[:, :, LANES:2 * LANES] = jnp.ones((nh, vx_s.shape[1], LANES), BF16)
        reset_max()
        acc_s[...] = jnp.zeros(acc_s.shape, F32)
        stack_q(q_ref[0, 0:tq, :])
        for c in chunks:
            put_scores(c, k_ref[0, 0:diag_cols(c), head_lanes(c)])

    def full_stage(kt, next_is_diag):
        start = pl.multiple_of(kt * tq, tq)
        for c in chunks:
            s = s_s[c:c + rc, :]
            put_scores(c, k_ref[0, pl.ds(start + tq, diag_cols(c) if next_is_diag else tq),
                                head_lanes(c)])
            update(c, s, start, tq, False)

    def body(kt, carry):
        full_stage(kt, False)
        return carry

    lax.fori_loop(0, qi - 1, body, 0)
    pl.when(qi > 0)(lambda: full_stage(qi - 1, True))

    def diagonal_stage(last):
        diag = pl.multiple_of(qi * tq, tq)
        stack_q(qn_ref[0] if last else q_ref[0, pl.ds(diag + tq, tq), :])

        def fetch(c):
            put_scores(c, kn_ref[0, 0:diag_cols(c), head_lanes(c)] if last
                       else k_ref[0, 0:tq, head_lanes(c)])

        order = sorted(chunks, key=diag_cols, reverse=True)
        late = order[-2:]
        for c in order:
            s = s_s[c:c + rc, 0:diag_cols(c)]
            if c not in late:
                fetch(c)
            update(c, s, diag, diag_cols(c), True)
        for c in late:
            fetch(c)
        lf = lam_ref[0]
        lam_full = (jnp.exp(jnp.sum(lf[0:1] * lf[1:2], axis=-1, keepdims=True))
                    - jnp.exp(jnp.sum(lf[2:3] * lf[3:4], axis=-1, keepdims=True))
                    + lambda_init)
        for j in range(nh):
            a0, a1 = j * rows, j * rows + tq
            o = (acc_s[a0:a0 + tq, 0:LANES] / acc_s[a0:a0 + tq, LANES:2 * LANES]
                 - lam_full * (acc_s[a1:a1 + tq, 0:LANES] / acc_s[a1:a1 + tq, LANES:2 * LANES]))
            o_ref[0, :, j * LANES:(j + 1) * LANES] = (
                _rms(o, g_ref[...]) * (1.0 - lambda_init)).astype(BF16)
        reset_max()

    pl.when(qi < nq - 1)(functools.partial(diagonal_stage, False))
    pl.when(qi == nq - 1)(functools.partial(diagonal_stage, True))


def _diff_attn(proj3d, lam, subln_g_row, lambda_init, *, tq=1024, rc=256, nh=2):
    b, s, _ = proj3d.shape
    width = nh * LANES
    groups = DA_HEADS // nh
    qb, kb, vb = _Q_OFF // width, _K_OFF // width, _V_OFF // width

    def next_group(i, h):
        n = jnp.minimum(i * groups + h + 1, b * groups - 1)
        return n // groups, n % groups

    def next_tile0(col0):
        def index_map(i, h, t):
            ni, nh_ = next_group(i, h)
            return ni, 0, col0 + nh_
        return index_map

    return pl.pallas_call(
        functools.partial(_diff_attn_kernel, tq=tq, rc=rc, nq=s // tq, nh=nh,
                          lambda_init=lambda_init),
        grid=(b, groups, s // tq),
        in_specs=[
            pl.BlockSpec((1, 4, DA_QK_DIM), lambda i, h, t: (0, 0, 0)),
            pl.BlockSpec((1, s, width), lambda i, h, t: (i, 0, qb + h)),
            pl.BlockSpec((1, s, width), lambda i, h, t: (i, 0, kb + h)),
            pl.BlockSpec((1, s, width), lambda i, h, t: (i, 0, vb + h)),
            pl.BlockSpec((1, tq, width), next_tile0(qb)),
            pl.BlockSpec((1, tq, width), next_tile0(kb)),
            pl.BlockSpec((1, LANES), lambda i, h, t: (0, 0)),
        ],
        out_specs=pl.BlockSpec((1, tq, width), lambda i, h, t: (i, t, h)),
        out_shape=jax.ShapeDtypeStruct((b, s, MIX_WIDTH), BF16),
        scratch_shapes=[
            pltpu.VMEM((nh * 2 * tq, LANES), BF16),
            pltpu.VMEM((nh, s, 2 * LANES), BF16),
            pltpu.VMEM((nh * 2 * tq, tq), F32),
            pltpu.VMEM((nh * 2 * tq, LANES), F32),
            pltpu.VMEM((nh * 2 * tq, 2 * LANES), F32),
        ],
        compiler_params=pltpu.CompilerParams(
            dimension_semantics=("arbitrary", "arbitrary", "arbitrary"),
            vmem_limit_bytes=VMEM_LIMIT_BYTES),
        name="diff_attn",
    )(lam, proj3d, proj3d, proj3d, proj3d, proj3d, subln_g_row)


def _mem_attention(qm, kv_ref, z_s, sgate_m, rows):
    for hd in range(MEM_HEADS):
        lo = hd * MEM_HEAD_DIM
        k = kv_ref[0, 0, :, lo:lo + MEM_HEAD_DIM]
        v = kv_ref[0, 0, :, MEM_WIDTH + lo:MEM_WIDTH + lo + MEM_HEAD_DIM]
        s = lax.dot_general(qm[:, lo:lo + MEM_HEAD_DIM], k, (((1,), (1,)), ((), ())),
                            preferred_element_type=F32) * (MEM_HEAD_DIM ** -0.5)
        p = jnp.exp(s - jnp.max(s, axis=-1, keepdims=True))
        l = jnp.sum(p, axis=-1, keepdims=True)
        m = jnp.dot(p.astype(BF16), v, preferred_element_type=F32) / l
        z_s[rows, MIX_WIDTH + lo:MIX_WIDTH + lo + MEM_HEAD_DIM] = (
            m * sgate_m[:, lo:lo + MEM_HEAD_DIM].astype(F32)).astype(BF16)


def _tail0_kernel(y_ref, sga_ref, sgb_ref, qm_ref, h_ref, kv_ref, wo_ref, o_ref, z_s, *, sub):
    n_b = MIX_WIDTH - GATE_BLOCK
    for r0 in range(0, z_s.shape[0], sub):
        rows = pl.ds(r0, sub)
        z_s[rows, 0:GATE_BLOCK] = y_ref[rows, 0:GATE_BLOCK] * sga_ref[rows, :]
        z_s[rows, GATE_BLOCK:MIX_WIDTH] = y_ref[rows, GATE_BLOCK:MIX_WIDTH] * sgb_ref[rows, 0:n_b]
        _mem_attention(qm_ref[rows, :], kv_ref, z_s, sgb_ref[rows, n_b:GATE_BLOCK], rows)
        o_ref[rows, :] = h_ref[rows, :] + jnp.dot(z_s[rows, :], wo_ref[0],
                                                  preferred_element_type=F32)


def _tail0(y2d, proj2d, h2d, kv0, wo_bf16, *, seq, tm=1024, sub=512):
    t, d = h2d.shape
    nt = seq // tm
    return pl.pallas_call(
        functools.partial(_tail0_kernel, sub=sub),
        grid=(t // tm,),
        in_specs=[
            pl.BlockSpec((tm, MIX_WIDTH), lambda i: (i, 0)),
            pl.BlockSpec((tm, GATE_BLOCK), lambda i: (i, _GATE_OFF // GATE_BLOCK)),
            pl.BlockSpec((tm, GATE_BLOCK), lambda i: (i, _GATE_OFF // GATE_BLOCK + 1)),
            pl.BlockSpec((tm, MEM_WIDTH), lambda i: (i, _QM_OFF // MEM_WIDTH)),
            pl.BlockSpec((tm, d), lambda i: (i, 0)),
            pl.BlockSpec((1, 1) + kv0.shape[2:], lambda i: (0, i // nt, 0, 0)),
            pl.BlockSpec((1, BRANCH_WIDTH, d), lambda i: (0, 0, 0), pipeline_mode=pl.Buffered(1)),
        ],
        out_specs=pl.BlockSpec((tm, d), lambda i: (i, 0)),
        out_shape=jax.ShapeDtypeStruct((t, d), F32),
        scratch_shapes=[pltpu.VMEM((tm, BRANCH_WIDTH), BF16)],
        compiler_params=pltpu.CompilerParams(
            dimension_semantics=("arbitrary",),
            vmem_limit_bytes=VMEM_LIMIT_BYTES),
        name="tail0",
    )(y2d, proj2d, proj2d, proj2d, h2d, kv0, wo_bf16)


def _layer1_kernel(h_ref, g_ref, win_ref, wg_ref, ps_ref, kv_ref, wo_ref, fg_ref, o_ref,
                   u_s, z_s, *, tm, sub):
    t_idx = pl.program_id(1)

    @pl.when(t_idx == 0)
    def _():
        u_s[0:POOL_HALO, :] = jnp.zeros((POOL_HALO, MIX_WIDTH), F32)

    @pl.when(t_idx > 0)
    def _():
        u_s[0:POOL_HALO, :] = u_s[tm:tm + POOL_HALO, :]

    for r0 in range(0, tm, sub):
        rows = pl.ds(r0, sub)
        h = h_ref[rows, :]
        hn = _rms(h, g_ref[...]).astype(BF16)
        u0 = POOL_HALO + r0
        u_s[u0:u0 + sub, :] = jnp.dot(hn, win_ref[:, 0:MIX_WIDTH], preferred_element_type=F32)
        qm = jnp.dot(hn, win_ref[:, MIX_WIDTH:MIX_WIDTH + MEM_WIDTH],
                     preferred_element_type=F32).astype(BF16)
        gate = jnp.dot(hn, win_ref[:, MIX_WIDTH + MEM_WIDTH:], preferred_element_type=F32)

        pos = t_idx * tm + r0 + lax.broadcasted_iota(jnp.int32, (sub, 1), 0)
        for gi, win in enumerate(POOL_WINDOWS):
            lo = gi * POOL_GROUP_WIDTH
            hi = lo + POOL_GROUP_WIDTH
            w = u_s[u0 - POOL_HALO:u0 + sub, lo:hi]
            ug = w[POOL_HALO:, :]
            step = 1
            while step < win:
                w = w + pltpu.roll(w, step, 0)
                step *= 2
            wsum = w[POOL_HALO:, :]
            inv_cnt = 1.0 / jnp.minimum(pos + 1, win).astype(F32)
            pooled = (wsum * inv_cnt - ug).astype(BF16)
            mixed = jnp.dot(pooled, wg_ref[gi], preferred_element_type=F32) * ps_ref[:, lo:hi]
            z_s[rows, lo:hi] = (mixed * _silu(gate[:, lo:hi])).astype(BF16)

        _mem_attention(qm, kv_ref, z_s, _silu(gate[:, MIX_WIDTH:]), rows)
        h2 = h + jnp.dot(z_s[rows, :], wo_ref[0], preferred_element_type=F32)
        o_ref[rows, :] = _rms(h2, fg_ref[...])


def _layer1(h2d, g_row, win_bf16, wg_bf16, ps_row, kv1, wo_bf16, fg_row, *, seq, tm=1024, sub=512):
    t, d = h2d.shape
    b = t // seq
    nt = seq // tm
    const2 = lambda i, j: (0, 0)
    resident = dict(pipeline_mode=pl.Buffered(1))
    return pl.pallas_call(
        functools.partial(_layer1_kernel, tm=tm, sub=sub),
        grid=(b, nt),
        in_specs=[
            pl.BlockSpec((tm, d), lambda i, j: (i * nt + j, 0)),
            pl.BlockSpec((1, d), const2),
            pl.BlockSpec((d, POOL_IN_WIDTH), const2, **resident),
            pl.BlockSpec(wg_bf16.shape, lambda i, j: (0, 0, 0), **resident),
            pl.BlockSpec((1, MIX_WIDTH), const2),
            pl.BlockSpec((1, 1) + kv1.shape[2:], lambda i, j: (1, i, 0, 0)),
            pl.BlockSpec((1, BRANCH_WIDTH, d), lambda i, j: (1, 0, 0), **resident),
            pl.BlockSpec((1, d), const2),
        ],
        out_specs=pl.BlockSpec((tm, d), lambda i, j: (i * nt + j, 0)),
        out_shape=jax.ShapeDtypeStruct((t, d), F32),
        scratch_shapes=[
            pltpu.VMEM((POOL_HALO + tm, MIX_WIDTH), F32),
            pltpu.VMEM((tm, BRANCH_WIDTH), BF16),
        ],
        compiler_params=pltpu.CompilerParams(
            dimension_semantics=("arbitrary", "arbitrary"),
            vmem_limit_bytes=VMEM_LIMIT_BYTES),
        name="layer1",
    )(h2d, g_row, win_bf16, wg_bf16, ps_row, kv1, wo_bf16, fg_row)


def kernel(x, mem, positions, ln_g, attn_w_in, attn_lambda, attn_subln_g, pool_w_in,
           pool_w_group, pool_scale, mem_norm_g, mem_w_kv, w_out, final_g):
    b, s, d = x.shape
    t = b * s
    lambda_init0 = 0.8 - 0.6 * math.exp(-0.3 * 0)

    w0 = attn_w_in[0].astype(BF16)
    wkv = mem_w_kv.astype(BF16)
    inv = ROPE_THETA ** (-jnp.arange(0, DA_QK_DIM, 2, dtype=F32) / DA_QK_DIM)
    inv_row = jnp.tile(inv, LANES // inv.shape[0]).reshape(1, LANES)

    h2d = x.reshape(t, d)
    kv = _mem_kv(mem, mem_norm_g, wkv)

    n_grp, gw = pool_w_group.shape[1], pool_w_group.shape[2]
    proj, w1, wg, wo = _inproj0(
        h2d, positions.reshape(t, 1), inv_row, ln_g[0].reshape(1, d), w0,
        (pool_w_in[0], pool_w_group[0].reshape(n_grp * gw, gw), w_out.reshape(-1, d)))
    wg = wg.reshape(n_grp, gw, gw)
    wo = wo.reshape(w_out.shape)
    y = _diff_attn(proj.reshape(b, s, ATTN_IN_WIDTH), attn_lambda[0:1],
                   attn_subln_g[0].reshape(1, DA_V_DIM), lambda_init0)
    h1 = _tail0(y.reshape(t, MIX_WIDTH), proj, h2d, kv, wo, seq=s)
    out = _layer1(h1, ln_g[1].reshape(1, d), w1, wg, pool_scale[0].reshape(1, MIX_WIDTH),
                  kv, wo, final_g.reshape(1, d), seq=s)
    return out.reshape(b, s, d)
```

```python
import functools
import math

import jax
import jax.numpy as jnp
from jax import lax
from jax.experimental import pallas as pl
from jax.experimental.pallas import tpu as pltpu

F32 = jnp.float32
BF16 = jnp.bfloat16

D_MODEL = 1024
MEM_HEADS = 4
MEM_HEAD_DIM = 128
MEM_WIDTH = MEM_HEADS * MEM_HEAD_DIM
BRANCH_WIDTH = 2 * D_MODEL
MIX_WIDTH = BRANCH_WIDTH - MEM_WIDTH
DA_QK_DIM = 64
DA_V_DIM = 2 * DA_QK_DIM
DA_HEADS = MIX_WIDTH // DA_V_DIM
ROPE_THETA = 10000.0
POOL_WINDOWS = (2, 4, 8, 16)
POOL_GROUP_WIDTH = MIX_WIDTH // len(POOL_WINDOWS)
POOL_HALO = 16
EPS = 1e-6
ATTN_IN_WIDTH = 3 * MIX_WIDTH + MEM_WIDTH + BRANCH_WIDTH
POOL_IN_WIDTH = MIX_WIDTH + MEM_WIDTH + BRANCH_WIDTH

LANES = 128
VMEM_LIMIT_BYTES = 56 * 1024 * 1024

_Q_OFF = 0
_K_OFF = _Q_OFF + MIX_WIDTH
_V_OFF = _K_OFF + MIX_WIDTH
_QM_OFF = _V_OFF + MIX_WIDTH
_GATE_OFF = _QM_OFF + MEM_WIDTH
GATE_BLOCK = D_MODEL


def _rms(x, g):
    ms = jnp.mean(x * x, axis=-1, keepdims=True)
    return x * lax.rsqrt(ms + EPS) * g


def _silu(x):
    hx = 0.5 * x
    return hx + hx * jnp.tanh(hx)


def _mem_kv_kernel(mem_ref, g_ref, w_ref, side_ref, o_ref, side_o_ref):
    side_o_ref[...] = side_ref[...].astype(BF16)
    mem_n = _rms(mem_ref[0], g_ref[...]).astype(BF16)
    o_ref[0, 0] = jnp.dot(mem_n, w_ref[0].astype(BF16), preferred_element_type=F32).astype(BF16)


def _mem_kv(mem, mem_norm_g, w_kv, side_f32):
    depth = w_kv.shape[0]
    b, m, d = mem.shape
    steps = depth * b
    side_rows = side_f32.shape[0] // steps
    assert side_rows * steps == side_f32.shape[0] and side_rows % 16 == 0
    side_spec = pl.BlockSpec((side_rows, side_f32.shape[1]), lambda l, i: (l * b + i, 0))
    return pl.pallas_call(
        _mem_kv_kernel,
        grid=(depth, b),
        in_specs=[
            pl.BlockSpec((1, m, d), lambda l, i: (i, 0, 0)),
            pl.BlockSpec((1, d), lambda l, i: (0, 0)),
            pl.BlockSpec((1, d, 2 * MEM_WIDTH), lambda l, i: (l, 0, 0)),
            side_spec,
        ],
        out_specs=[pl.BlockSpec((1, 1, m, 2 * MEM_WIDTH), lambda l, i: (l, i, 0, 0)), side_spec],
        out_shape=[jax.ShapeDtypeStruct((depth, b, m, 2 * MEM_WIDTH), BF16),
                   jax.ShapeDtypeStruct(side_f32.shape, BF16)],
        compiler_params=pltpu.CompilerParams(
            dimension_semantics=("arbitrary", "arbitrary"),
            vmem_limit_bytes=VMEM_LIMIT_BYTES),
        name="mem_kv",
    )(mem, mem_norm_g.reshape(1, d), w_kv, side_f32)


def _inproj0_kernel(pos_ref, inv_ref, x_ref, g_ref, w_ref, *refs, tn, n_side):
    side_in, o_ref, side_out = refs[:n_side], refs[n_side], refs[n_side + 1:2 * n_side + 1]
    hn_s, cos_s, sin_s = refs[2 * n_side + 1:]
    for src, dst in zip(side_in, side_out):
        dst[...] = src[...].astype(BF16)

    lane = lax.broadcasted_iota(jnp.int32, (1, LANES), 1)
    first_half = (lane % DA_QK_DIM) < (DA_QK_DIM // 2)
    hn_s[...] = _rms(x_ref[...], g_ref[...]).astype(BF16)
    n_grp = LANES // (DA_QK_DIM // 2)
    grp = lane // (DA_QK_DIM // 2)
    rb = pos_ref.shape[0] // n_grp
    ang = None
    for g in range(n_grp):
        a = pos_ref[g * rb:(g + 1) * rb, :].astype(F32) * inv_ref[...]
        ang = a if ang is None else jnp.where(grp == g, a, ang)
    for packed, dst, signed in ((jnp.cos(ang), cos_s, False), (jnp.sin(ang), sin_s, True)):
        shifted = [packed] + [pltpu.roll(packed, k * (DA_QK_DIM // 2), 1) for k in range(1, n_grp)]
        for g in range(n_grp):
            tbl = shifted[0]
            for k in range(1, n_grp):
                tbl = jnp.where(grp == (g + k) % n_grp, shifted[k], tbl)
            if signed:
                tbl = jnp.where(first_half, -tbl, tbl)
            dst[g * rb:(g + 1) * rb, :] = tbl

    qscale = DA_QK_DIM ** -0.5 * math.log2(math.e)
    for lo in sorted(range(0, w_ref.shape[1], tn), key=lambda c: c < _GATE_OFF):
        acc = jnp.dot(hn_s[...], w_ref[:, lo:lo + tn], preferred_element_type=F32)
        if _Q_OFF <= lo < _V_OFF:
            scale = qscale if lo < _K_OFF else 1.0
            cos = cos_s[...] * scale
            sin = sin_s[...] * scale
            for c in range(0, tn, LANES):
                xc = acc[:, c:c + LANES]
                rot = jnp.where(first_half,
                                pltpu.roll(xc, LANES - DA_QK_DIM // 2, 1),
                                pltpu.roll(xc, DA_QK_DIM // 2, 1))
                o_ref[:, lo + c:lo + c + LANES] = (xc * cos + rot * sin).astype(BF16)
        elif lo >= _GATE_OFF:
            o_ref[:, lo:lo + tn] = _silu(acc).astype(BF16)
        else:
            o_ref[:, lo:lo + tn] = acc.astype(BF16)


def _inproj0(h2d, pos2d, inv_row, g_row, w_bf16, side_f32, *, tm=512, tn=512):
    t, d = h2d.shape
    n = w_bf16.shape[1]
    steps = t // tm
    side_blocks = [(a.shape[0] // steps, a.shape[1]) for a in side_f32]
    assert all(blk[0] * steps == a.shape[0] and blk[0] % 16 == 0
               for blk, a in zip(side_blocks, side_f32))
    side_specs = [pl.BlockSpec(blk, lambda i: (i, 0)) for blk in side_blocks]
    return pl.pallas_call(
        functools.partial(_inproj0_kernel, tn=tn, n_side=len(side_f32)),
        grid=(steps,),
        in_specs=[
            pl.BlockSpec((tm, 1), lambda i: (i, 0)),
            pl.BlockSpec((1, LANES), lambda i: (0, 0)),
            pl.BlockSpec((tm, d), lambda i: (i, 0)),
            pl.BlockSpec((1, d), lambda i: (0, 0)),
            pl.BlockSpec((d, n), lambda i: (0, 0), pipeline_mode=pl.Buffered(1)),
        ] + side_specs,
        out_specs=[pl.BlockSpec((tm, n), lambda i: (i, 0))] + side_specs,
        out_shape=[jax.ShapeDtypeStruct((t, n), BF16)]
        + [jax.ShapeDtypeStruct(a.shape, BF16) for a in side_f32],
        scratch_shapes=[
            pltpu.VMEM((tm, d), BF16),
            pltpu.VMEM((tm, LANES), F32),
            pltpu.VMEM((tm, LANES), F32),
        ],
        compiler_params=pltpu.CompilerParams(
            dimension_semantics=("arbitrary",),
            vmem_limit_bytes=VMEM_LIMIT_BYTES),
        name="inproj0",
    )(pos2d, inv_row, h2d, g_row, w_bf16, *side_f32)


def _diff_attn_kernel(lam_ref, q_ref, k_ref, v_ref, qn_ref, kn_ref, g_ref, o_ref,
                      qs_s, vx_s, s_s, m_s, acc_s, *, tq, rc, nq, nh, lambda_init):
    qi = pl.program_id(2)
    rows = 2 * tq
    chunks = [j * rows + r for r in range(0, rows, rc) for j in range(nh)]

    def head_lanes(c):
        j = c // rows
        return slice(j * LANES, (j + 1) * LANES)

    def stack_q(q):
        lane = lax.broadcasted_iota(jnp.int32, (tq, LANES), 1)
        for j in range(nh):
            qj = q[:, j * LANES:(j + 1) * LANES]
            zero = jnp.zeros_like(qj)
            qs_s[j * rows:j * rows + tq, :] = jnp.where(lane < DA_QK_DIM, qj, zero)
            qs_s[j * rows + tq:(j + 1) * rows, :] = jnp.where(lane >= DA_QK_DIM, qj, zero)

    def reset_max():
        m_s[...] = jnp.full(m_s.shape, -jnp.inf, F32)

    def diag_cols(c):
        return c % tq + rc

    def put_scores(c, k):
        ncols = k.shape[0]
        s_s[c:c + rc, 0:ncols] = lax.dot_general(qs_s[c:c + rc, :], k, (((1,), (1,)), ((), ())),
                                                 preferred_element_type=F32)

    def update(c, s, start, ncols, masked):
        vx = vx_s[c // rows, pl.ds(start, ncols), :]
        cols = [s[:, i:i + LANES] for i in range(0, ncols, LANES)]
        if masked:
            row = lax.broadcasted_iota(jnp.int32, (rc, LANES), 0)
            col = lax.broadcasted_iota(jnp.int32, (rc, LANES), 1)
            for i in range(rc // LANES):
                ci = (ncols - rc) // LANES + i
                cols[ci] = jnp.where(col + i * LANES <= row, cols[ci], -jnp.inf)
        m_prev = m_s[c:c + rc, :]
        m_cur = cols[0]
        for x in cols[1:]:
            m_cur = jnp.maximum(m_cur, x)
        m_new = jnp.maximum(m_prev, jnp.max(m_cur, axis=-1, keepdims=True))
        alpha = jnp.exp2(m_prev - m_new)
        p = jnp.concatenate([jnp.exp2(x - m_new) for x in cols], axis=1).astype(BF16)
        pv = jnp.dot(p, vx, preferred_element_type=F32)
        acc_s[c:c + rc, :] = jnp.concatenate([alpha, alpha], axis=1) * acc_s[c:c + rc, :] + pv
        m_s[c:c + rc, :] = m_new

    @pl.when(qi == 0)
    def _():
        for j in range(nh):
            vx_s[j, :, 0:LANES] = v_ref[0, :, j * LANES:(j + 1) * LANES]

    @pl.when((pl.program_id(0) == 0) & (pl.program_id(1) == 0) & (qi == 0))
    def _():
        vx_s[:, :, LANES:2 * LANES] = jnp.ones((nh, vx_s.shape[1], LANES), BF16)
        reset_max()
        acc_s[...] = jnp.zeros(acc_s.shape, F32)
        stack_q(q_ref[0, 0:tq, :])
        for c in chunks:
            put_scores(c, k_ref[0, 0:diag_cols(c), head_lanes(c)])

    def full_stage(kt, next_is_diag):
        start = pl.multiple_of(kt * tq, tq)
        for c in chunks:
            s = s_s[c:c + rc, :]
            put_scores(c, k_ref[0, pl.ds(start + tq, diag_cols(c) if next_is_diag else tq),
                                head_lanes(c)])
            update(c, s, start, tq, False)

    def body(kt, carry):
        full_stage(kt, False)
        return carry

    lax.fori_loop(0, qi - 1, body, 0)
    pl.when(qi > 0)(lambda: full_stage(qi - 1, True))

    def diagonal_stage(last):
        diag = pl.multiple_of(qi * tq, tq)
        stack_q(qn_ref[0] if last else q_ref[0, pl.ds(diag + tq, tq), :])

        def fetch(c):
            put_scores(c, kn_ref[0, 0:diag_cols(c), head_lanes(c)] if last
                       else k_ref[0, 0:tq, head_lanes(c)])

        order = sorted(chunks, key=diag_cols, reverse=True)
        late = order[-2:]
        for c in order:
            s = s_s[c:c + rc, 0:diag_cols(c)]
            if c not in late:
                fetch(c)
            update(c, s, diag, diag_cols(c), True)
        for c in late:
            fetch(c)
        lf = lam_ref[0]
        lam_full = (jnp.exp(jnp.sum(lf[0:1] * lf[1:2], axis=-1, keepdims=True))
                    - jnp.exp(jnp.sum(lf[2:3] * lf[3:4], axis=-1, keepdims=True))
                    + lambda_init)
        for j in range(nh):
            a0, a1 = j * rows, j * rows + tq
            o = (acc_s[a0:a0 + tq, 0:LANES] / acc_s[a0:a0 + tq, LANES:2 * LANES]
                 - lam_full * (acc_s[a1:a1 + tq, 0:LANES] / acc_s[a1:a1 + tq, LANES:2 * LANES]))
            o_ref[0, :, j * LANES:(j + 1) * LANES] = (
                _rms(o, g_ref[...]) * (1.0 - lambda_init)).astype(BF16)
        reset_max()

    pl.when(qi < nq - 1)(functools.partial(diagonal_stage, False))
    pl.when(qi == nq - 1)(functools.partial(diagonal_stage, True))


def _diff_attn(proj3d, lam, subln_g_row, lambda_init, *, tq=1024, rc=256, nh=2):
    b, s, _ = proj3d.shape
    width = nh * LANES
    groups = DA_HEADS // nh
    qb, kb, vb = _Q_OFF // width, _K_OFF // width, _V_OFF // width

    def next_group(i, h):
        n = jnp.minimum(i * groups + h + 1, b * groups - 1)
        return n // groups, n % groups

    def next_tile0(col0):
        def index_map(i, h, t):
            ni, nh_ = next_group(i, h)
            return ni, 0, col0 + nh_
        return index_map

    return pl.pallas_call(
        functools.partial(_diff_attn_kernel, tq=tq, rc=rc, nq=s // tq, nh=nh,
                          lambda_init=lambda_init),
        grid=(b, groups, s // tq),
        in_specs=[
            pl.BlockSpec((1, 4, DA_QK_DIM), lambda i, h, t: (0, 0, 0)),
            pl.BlockSpec((1, s, width), lambda i, h, t: (i, 0, qb + h)),
            pl.BlockSpec((1, s, width), lambda i, h, t: (i, 0, kb + h)),
            pl.BlockSpec((1, s, width), lambda i, h, t: (i, 0, vb + h)),
            pl.BlockSpec((1, tq, width), next_tile0(qb)),
            pl.BlockSpec((1, tq, width), next_tile0(kb)),
            pl.BlockSpec((1, LANES), lambda i, h, t: (0, 0)),
        ],
        out_specs=pl.BlockSpec((1, tq, width), lambda i, h, t: (i, t, h)),
        out_shape=jax.ShapeDtypeStruct((b, s, MIX_WIDTH), BF16),
        scratch_shapes=[
            pltpu.VMEM((nh * 2 * tq, LANES), BF16),
            pltpu.VMEM((nh, s, 2 * LANES), BF16),
            pltpu.VMEM((nh * 2 * tq, tq), F32),
            pltpu.VMEM((nh * 2 * tq, LANES), F32),
            pltpu.VMEM((nh * 2 * tq, 2 * LANES), F32),
        ],
        compiler_params=pltpu.CompilerParams(
            dimension_semantics=("arbitrary", "arbitrary", "arbitrary"),
            vmem_limit_bytes=VMEM_LIMIT_BYTES),
        name="diff_attn",
    )(lam, proj3d, proj3d, proj3d, proj3d, proj3d, subln_g_row)


def _mem_attention(qm, kv_ref, z_s, sgate_m, rows):
    for hd in range(MEM_HEADS):
        lo = hd * MEM_HEAD_DIM
        k = kv_ref[0, 0, :, lo:lo + MEM_HEAD_DIM]
        v = kv_ref[0, 0, :, MEM_WIDTH + lo:MEM_WIDTH + lo + MEM_HEAD_DIM]
        s = lax.dot_general(qm[:, lo:lo + MEM_HEAD_DIM], k, (((1,), (1,)), ((), ())),
                            preferred_element_type=F32) * (MEM_HEAD_DIM ** -0.5)
        p = jnp.exp(s - jnp.max(s, axis=-1, keepdims=True))
        l = jnp.sum(p, axis=-1, keepdims=True)
        m = jnp.dot(p.astype(BF16), v, preferred_element_type=F32) / l
        z_s[rows, MIX_WIDTH + lo:MIX_WIDTH + lo + MEM_HEAD_DIM] = (
            m * sgate_m[:, lo:lo + MEM_HEAD_DIM].astype(F32)).astype(BF16)


def _tail0_kernel(y_ref, sga_ref, sgb_ref, qm_ref, h_ref, kv_ref, wo_ref, o_ref, z_s, *, sub):
    n_b = MIX_WIDTH - GATE_BLOCK
    for r0 in range(0, z_s.shape[0], sub):
        rows = pl.ds(r0, sub)
        z_s[rows, 0:GATE_BLOCK] = y_ref[rows, 0:GATE_BLOCK] * sga_ref[rows, :]
        z_s[rows, GATE_BLOCK:MIX_WIDTH] = y_ref[rows, GATE_BLOCK:MIX_WIDTH] * sgb_ref[rows, 0:n_b]
        _mem_attention(qm_ref[rows, :], kv_ref, z_s, sgb_ref[rows, n_b:GATE_BLOCK], rows)
        o_ref[rows, :] = h_ref[rows, :] + jnp.dot(z_s[rows, :], wo_ref[0],
                                                  preferred_element_type=F32)


def _tail0(y2d, proj2d, h2d, kv0, wo_bf16, *, seq, tm=1024, sub=512):
    t, d = h2d.shape
    nt = seq // tm
    return pl.pallas_call(
        functools.partial(_tail0_kernel, sub=sub),
        grid=(t // tm,),
        in_specs=[
            pl.BlockSpec((tm, MIX_WIDTH), lambda i: (i, 0)),
            pl.BlockSpec((tm, GATE_BLOCK), lambda i: (i, _GATE_OFF // GATE_BLOCK)),
            pl.BlockSpec((tm, GATE_BLOCK), lambda i: (i, _GATE_OFF // GATE_BLOCK + 1)),
            pl.BlockSpec((tm, MEM_WIDTH), lambda i: (i, _QM_OFF // MEM_WIDTH)),
            pl.BlockSpec((tm, d), lambda i: (i, 0)),
            pl.BlockSpec((1, 1) + kv0.shape[2:], lambda i: (0, i // nt, 0, 0)),
            pl.BlockSpec((1, BRANCH_WIDTH, d), lambda i: (0, 0, 0), pipeline_mode=pl.Buffered(1)),
        ],
        out_specs=pl.BlockSpec((tm, d), lambda i: (i, 0)),
        out_shape=jax.ShapeDtypeStruct((t, d), F32),
        scratch_shapes=[pltpu.VMEM((tm, BRANCH_WIDTH), BF16)],
        compiler_params=pltpu.CompilerParams(
            dimension_semantics=("arbitrary",),
            vmem_limit_bytes=VMEM_LIMIT_BYTES),
        name="tail0",
    )(y2d, proj2d, proj2d, proj2d, h2d, kv0, wo_bf16)


def _layer1_kernel(h_ref, g_ref, win_ref, wg_ref, ps_ref, kv_ref, wo_ref, fg_ref, o_ref,
                   u_s, z_s, *, tm, sub):
    t_idx = pl.program_id(1)

    @pl.when(t_idx == 0)
    def _():
        u_s[0:POOL_HALO, :] = jnp.zeros((POOL_HALO, MIX_WIDTH), F32)

    @pl.when(t_idx > 0)
    def _():
        u_s[0:POOL_HALO, :] = u_s[tm:tm + POOL_HALO, :]

    for r0 in range(0, tm, sub):
        rows = pl.ds(r0, sub)
        h = h_ref[rows, :]
        hn = _rms(h, g_ref[...]).astype(BF16)
        u0 = POOL_HALO + r0
        u_s[u0:u0 + sub, :] = jnp.dot(hn, win_ref[:, 0:MIX_WIDTH], preferred_element_type=F32)
        qm = jnp.dot(hn, win_ref[:, MIX_WIDTH:MIX_WIDTH + MEM_WIDTH],
                     preferred_element_type=F32).astype(BF16)
        gate = jnp.dot(hn, win_ref[:, MIX_WIDTH + MEM_WIDTH:], preferred_element_type=F32)

        pos = t_idx * tm + r0 + lax.broadcasted_iota(jnp.int32, (sub, 1), 0)
        for gi, win in enumerate(POOL_WINDOWS):
            lo = gi * POOL_GROUP_WIDTH
            hi = lo + POOL_GROUP_WIDTH
            w = u_s[u0 - POOL_HALO:u0 + sub, lo:hi]
            ug = w[POOL_HALO:, :]
            step = 1
            while step < win:
                w = w + pltpu.roll(w, step, 0)
                step *= 2
            wsum = w[POOL_HALO:, :]
            inv_cnt = 1.0 / jnp.minimum(pos + 1, win).astype(F32)
            pooled = (wsum * inv_cnt - ug).astype(BF16)
            mixed = jnp.dot(pooled, wg_ref[gi], preferred_element_type=F32) * ps_ref[:, lo:hi]
            z_s[rows, lo:hi] = (mixed * _silu(gate[:, lo:hi])).astype(BF16)

        _mem_attention(qm, kv_ref, z_s, _silu(gate[:, MIX_WIDTH:]), rows)
        h2 = h + jnp.dot(z_s[rows, :], wo_ref[0], preferred_element_type=F32)
        o_ref[rows, :] = _rms(h2, fg_ref[...])


def _layer1(h2d, g_row, win_bf16, wg_bf16, ps_row, kv1, wo_bf16, fg_row, *, seq, tm=1024, sub=512):
    t, d = h2d.shape
    b = t // seq
    nt = seq // tm
    const2 = lambda i, j: (0, 0)
    resident = dict(pipeline_mode=pl.Buffered(1))
    return pl.pallas_call(
        functools.partial(_layer1_kernel, tm=tm, sub=sub),
        grid=(b, nt),
        in_specs=[
            pl.BlockSpec((tm, d), lambda i, j: (i * nt + j, 0)),
            pl.BlockSpec((1, d), const2),
            pl.BlockSpec((d, POOL_IN_WIDTH), const2, **resident),
            pl.BlockSpec(wg_bf16.shape, lambda i, j: (0, 0, 0), **resident),
            pl.BlockSpec((1, MIX_WIDTH), const2),
            pl.BlockSpec((1, 1) + kv1.shape[2:], lambda i, j: (1, i, 0, 0)),
            pl.BlockSpec((1, BRANCH_WIDTH, d), lambda i, j: (1, 0, 0), **resident),
            pl.BlockSpec((1, d), const2),
        ],
        out_specs=pl.BlockSpec((tm, d), lambda i, j: (i * nt + j, 0)),
        out_shape=jax.ShapeDtypeStruct((t, d), F32),
        scratch_shapes=[
            pltpu.VMEM((POOL_HALO + tm, MIX_WIDTH), F32),
            pltpu.VMEM((tm, BRANCH_WIDTH), BF16),
        ],
        compiler_params=pltpu.CompilerParams(
            dimension_semantics=("arbitrary", "arbitrary"),
            vmem_limit_bytes=VMEM_LIMIT_BYTES),
        name="layer1",
    )(h2d, g_row, win_bf16, wg_bf16, ps_row, kv1, wo_bf16, fg_row)


def kernel(x, mem, positions, ln_g, attn_w_in, attn_lambda, attn_subln_g, pool_w_in,
           pool_w_group, pool_scale, mem_norm_g, mem_w_kv, w_out, final_g):
    b, s, d = x.shape
    t = b * s
    lambda_init0 = 0.8 - 0.6 * math.exp(-0.3 * 0)

    inv = ROPE_THETA ** (-jnp.arange(0, DA_QK_DIM, 2, dtype=F32) / DA_QK_DIM)
    inv_row = jnp.tile(inv, LANES // inv.shape[0]).reshape(1, LANES)

    h2d = x.reshape(t, d)
    kv, w0 = _mem_kv(mem, mem_norm_g, mem_w_kv, attn_w_in[0])

    n_grp, gw = pool_w_group.shape[1], pool_w_group.shape[2]
    proj, w1, wg, wo = _inproj0(
        h2d, positions.reshape(t, 1), inv_row, ln_g[0].reshape(1, d), w0,
        (pool_w_in[0], pool_w_group[0].reshape(n_grp * gw, gw), w_out.reshape(-1, d)))
    wg = wg.reshape(n_grp, gw, gw)
    wo = wo.reshape(w_out.shape)
    y = _diff_attn(proj.reshape(b, s, ATTN_IN_WIDTH), attn_lambda[0:1],
                   attn_subln_g[0].reshape(1, DA_V_DIM), lambda_init0)
    h1 = _tail0(y.reshape(t, MIX_WIDTH), proj, h2d, kv, wo, seq=s)
    out = _layer1(h1, ln_g[1].reshape(1, d), w1, wg, pool_scale[0].reshape(1, MIX_WIDTH),
                  kv, wo, final_g.reshape(1, d), seq=s)
    return out.reshape(b, s, d)
```

```python
import functools
import math

import jax
import jax.numpy as jnp
from jax import lax
from jax.experimental import pallas as pl
from jax.experimental.pallas import tpu as pltpu

F32 = jnp.float32
BF16 = jnp.bfloat16

D_MODEL = 1024
MEM_HEADS = 4
MEM_HEAD_DIM = 128
MEM_WIDTH = MEM_HEADS * MEM_HEAD_DIM
BRANCH_WIDTH = 2 * D_MODEL
MIX_WIDTH = BRANCH_WIDTH - MEM_WIDTH
DA_QK_DIM = 64
DA_V_DIM = 2 * DA_QK_DIM
DA_HEADS = MIX_WIDTH // DA_V_DIM
ROPE_THETA = 10000.0
POOL_WINDOWS = (2, 4, 8, 16)
POOL_GROUP_WIDTH = MIX_WIDTH // len(POOL_WINDOWS)
POOL_HALO = 16
EPS = 1e-6
ATTN_IN_WIDTH = 3 * MIX_WIDTH + MEM_WIDTH + BRANCH_WIDTH
POOL_IN_WIDTH = MIX_WIDTH + MEM_WIDTH + BRANCH_WIDTH

LANES = 128
VMEM_LIMIT_BYTES = 56 * 1024 * 1024

_Q_OFF = 0
_K_OFF = _Q_OFF + MIX_WIDTH
_V_OFF = _K_OFF + MIX_WIDTH
_QM_OFF = _V_OFF + MIX_WIDTH
_GATE_OFF = _QM_OFF + MEM_WIDTH
GATE_BLOCK = D_MODEL


def _rms(x, g):
    ms = jnp.mean(x * x, axis=-1, keepdims=True)
    return x * lax.rsqrt(ms + EPS) * g


def _silu(x):
    hx = 0.5 * x
    return hx + hx * jnp.tanh(hx)


def _mem_kv_kernel(mem_ref, g_ref, w_ref, side_ref, o_ref, side_o_ref):
    side_o_ref[...] = side_ref[...].astype(BF16)
    mem_n = _rms(mem_ref[0], g_ref[...]).astype(BF16)
    o_ref[0, 0] = jnp.dot(mem_n, w_ref[0].astype(BF16), preferred_element_type=F32).astype(BF16)


def _mem_kv(mem, mem_norm_g, w_kv, side_f32):
    depth = w_kv.shape[0]
    b, m, d = mem.shape
    steps = depth * b
    side_rows = side_f32.shape[0] // steps
    assert side_rows * steps == side_f32.shape[0] and side_rows % 16 == 0
    side_spec = pl.BlockSpec((side_rows, side_f32.shape[1]), lambda l, i: (l * b + i, 0))
    return pl.pallas_call(
        _mem_kv_kernel,
        grid=(depth, b),
        in_specs=[
            pl.BlockSpec((1, m, d), lambda l, i: (i, 0, 0)),
            pl.BlockSpec((1, d), lambda l, i: (0, 0)),
            pl.BlockSpec((1, d, 2 * MEM_WIDTH), lambda l, i: (l, 0, 0)),
            side_spec,
        ],
        out_specs=[pl.BlockSpec((1, 1, m, 2 * MEM_WIDTH), lambda l, i: (l, i, 0, 0)), side_spec],
        out_shape=[jax.ShapeDtypeStruct((depth, b, m, 2 * MEM_WIDTH), BF16),
                   jax.ShapeDtypeStruct(side_f32.shape, BF16)],
        compiler_params=pltpu.CompilerParams(
            dimension_semantics=("arbitrary", "arbitrary"),
            vmem_limit_bytes=VMEM_LIMIT_BYTES),
        name="mem_kv",
    )(mem, mem_norm_g.reshape(1, d), w_kv, side_f32)


def _inproj0_kernel(pos_ref, inv_ref, x_ref, g_ref, w_ref, *refs, tn, n_side):
    side_in, o_ref, side_out = refs[:n_side], refs[n_side], refs[n_side + 1:2 * n_side + 1]
    hn_s, cos_s, sin_s = refs[2 * n_side + 1:]
    for src, dst in zip(side_in, side_out):
        dst[...] = src[...].astype(BF16)

    lane = lax.broadcasted_iota(jnp.int32, (1, LANES), 1)
    first_half = (lane % DA_QK_DIM) < (DA_QK_DIM // 2)
    hn_s[...] = _rms(x_ref[...], g_ref[...]).astype(BF16)
    n_grp = LANES // (DA_QK_DIM // 2)
    grp = lane // (DA_QK_DIM // 2)
    rb = pos_ref.shape[2]
    pos_rows = jnp.concatenate(
        [pos_ref[0].astype(F32), jnp.zeros((rb - n_grp, rb), F32)], axis=0)
    pos_cols = pos_rows.T
    ang = None
    for g in range(n_grp):
        a = pos_cols[:, g:g + 1] * inv_ref[...]
        ang = a if ang is None else jnp.where(grp == g, a, ang)
    for packed, dst, signed in ((jnp.cos(ang), cos_s, False), (jnp.sin(ang), sin_s, True)):
        shifted = [packed] + [pltpu.roll(packed, k * (DA_QK_DIM // 2), 1) for k in range(1, n_grp)]
        for g in range(n_grp):
            tbl = shifted[0]
            for k in range(1, n_grp):
                tbl = jnp.where(grp == (g + k) % n_grp, shifted[k], tbl)
            if signed:
                tbl = jnp.where(first_half, -tbl, tbl)
            dst[g * rb:(g + 1) * rb, :] = tbl

    qscale = DA_QK_DIM ** -0.5 * math.log2(math.e)
    for lo in sorted(range(0, w_ref.shape[1], tn), key=lambda c: c < _GATE_OFF):
        acc = jnp.dot(hn_s[...], w_ref[:, lo:lo + tn], preferred_element_type=F32)
        if _Q_OFF <= lo < _V_OFF:
            scale = qscale if lo < _K_OFF else 1.0
            cos = cos_s[...] * scale
            sin = sin_s[...] * scale
            for c in range(0, tn, LANES):
                xc = acc[:, c:c + LANES]
                rot = jnp.where(first_half,
                                pltpu.roll(xc, LANES - DA_QK_DIM // 2, 1),
                                pltpu.roll(xc, DA_QK_DIM // 2, 1))
                o_ref[:, lo + c:lo + c + LANES] = (xc * cos + rot * sin).astype(BF16)
        elif lo >= _GATE_OFF:
            o_ref[:, lo:lo + tn] = _silu(acc).astype(BF16)
        else:
            o_ref[:, lo:lo + tn] = acc.astype(BF16)


def _inproj0(h2d, pos3d, inv_row, g_row, w_bf16, side_f32, *, tm=512, tn=512):
    t, d = h2d.shape
    assert pos3d.shape[1] * pos3d.shape[2] == tm and pos3d.shape[2] == LANES
    n = w_bf16.shape[1]
    steps = t // tm
    side_blocks = [(a.shape[0] // steps, a.shape[1]) for a in side_f32]
    assert all(blk[0] * steps == a.shape[0] and blk[0] % 16 == 0
               for blk, a in zip(side_blocks, side_f32))
    side_specs = [pl.BlockSpec(blk, lambda i: (i, 0)) for blk in side_blocks]
    return pl.pallas_call(
        functools.partial(_inproj0_kernel, tn=tn, n_side=len(side_f32)),
        grid=(steps,),
        in_specs=[
            pl.BlockSpec((1,) + pos3d.shape[1:], lambda i: (i, 0, 0)),
            pl.BlockSpec((1, LANES), lambda i: (0, 0)),
            pl.BlockSpec((tm, d), lambda i: (i, 0)),
            pl.BlockSpec((1, d), lambda i: (0, 0)),
            pl.BlockSpec((d, n), lambda i: (0, 0), pipeline_mode=pl.Buffered(1)),
        ] + side_specs,
        out_specs=[pl.BlockSpec((tm, n), lambda i: (i, 0))] + side_specs,
        out_shape=[jax.ShapeDtypeStruct((t, n), BF16)]
        + [jax.ShapeDtypeStruct(a.shape, BF16) for a in side_f32],
        scratch_shapes=[
            pltpu.VMEM((tm, d), BF16),
            pltpu.VMEM((tm, LANES), F32),
            pltpu.VMEM((tm, LANES), F32),
        ],
        compiler_params=pltpu.CompilerParams(
            dimension_semantics=("arbitrary",),
            vmem_limit_bytes=VMEM_LIMIT_BYTES),
        name="inproj0",
    )(pos3d, inv_row, h2d, g_row, w_bf16, *side_f32)


def _diff_attn_kernel(lam_ref, q_ref, k_ref, v_ref, qn_ref, kn_ref, g_ref, o_ref,
                      qs_s, vx_s, s_s, m_s, acc_s, *, tq, rc, nq, nh, lambda_init):
    qi = pl.program_id(2)
    rows = 2 * tq
    chunks = [j * rows + r for r in range(0, rows, rc) for j in range(nh)]

    def head_lanes(c):
        j = c // rows
        return slice(j * LANES, (j + 1) * LANES)

    def stack_q(q):
        lane = lax.broadcasted_iota(jnp.int32, (tq, LANES), 1)
        for j in range(nh):
            qj = q[:, j * LANES:(j + 1) * LANES]
            zero = jnp.zeros_like(qj)
            qs_s[j * rows:j * rows + tq, :] = jnp.where(lane < DA_QK_DIM, qj, zero)
            qs_s[j * rows + tq:(j + 1) * rows, :] = jnp.where(lane >= DA_QK_DIM, qj, zero)

    def reset_max():
        m_s[...] = jnp.full(m_s.shape, -jnp.inf, F32)

    def diag_cols(c):
        return c % tq + rc

    def put_scores(c, k):
        ncols = k.shape[0]
        s_s[c:c + rc, 0:ncols] = lax.dot_general(qs_s[c:c + rc, :], k, (((1,), (1,)), ((), ())),
                                                 preferred_element_type=F32)

    def update(c, s, start, ncols, masked):
        vx = vx_s[c // rows, pl.ds(start, ncols), :]
        cols = [s[:, i:i + LANES] for i in range(0, ncols, LANES)]
        if masked:
            row = lax.broadcasted_iota(jnp.int32, (rc, LANES), 0)
            col = lax.broadcasted_iota(jnp.int32, (rc, LANES), 1)
            for i in range(rc // LANES):
                ci = (ncols - rc) // LANES + i
                cols[ci] = jnp.where(col + i * LANES <= row, cols[ci], -jnp.inf)
        m_prev = m_s[c:c + rc, :]
        m_cur = cols[0]
        for x in cols[1:]:
            m_cur = jnp.maximum(m_cur, x)
        m_new = jnp.maximum(m_prev, jnp.max(m_cur, axis=-1, keepdims=True))
        alpha = jnp.exp2(m_prev - m_new)
        p = jnp.concatenate([jnp.exp2(x - m_new) for x in cols], axis=1).astype(BF16)
        pv = jnp.dot(p, vx, preferred_element_type=F32)
        acc_s[c:c + rc, :] = jnp.concatenate([alpha, alpha], axis=1) * acc_s[c:c + rc, :] + pv
        m_s[c:c + rc, :] = m_new

    @pl.when(qi == 0)
    def _():
        for j in range(nh):
            vx_s[j, :, 0:LANES] = v_ref[0, :, j * LANES:(j + 1) * LANES]

    @pl.when((pl.program_id(0) == 0) & (pl.program_id(1) == 0) & (qi == 0))
    def _():
        vx_s[:, :, LANES:2 * LANES] = jnp.ones((nh, vx_s.shape[1], LANES), BF16)
        reset_max()
        acc_s[...] = jnp.zeros(acc_s.shape, F32)
        stack_q(q_ref[0, 0:tq, :])
        for c in chunks:
            put_scores(c, k_ref[0, 0:diag_cols(c), head_lanes(c)])

    def full_stage(kt, next_is_diag):
        start = pl.multiple_of(kt * tq, tq)
        for c in chunks:
            s = s_s[c:c + rc, :]
            put_scores(c, k_ref[0, pl.ds(start + tq, diag_cols(c) if next_is_diag else tq),
                                head_lanes(c)])
            update(c, s, start, tq, False)

    def body(kt, carry):
        full_stage(kt, False)
        return carry

    lax.fori_loop(0, qi - 1, body, 0)
    pl.when(qi > 0)(lambda: full_stage(qi - 1, True))

    def diagonal_stage(last):
        diag = pl.multiple_of(qi * tq, tq)
        stack_q(qn_ref[0] if last else q_ref[0, pl.ds(diag + tq, tq), :])

        def fetch(c):
            put_scores(c, kn_ref[0, 0:diag_cols(c), head_lanes(c)] if last
                       else k_ref[0, 0:tq, head_lanes(c)])

        order = sorted(chunks, key=diag_cols, reverse=True)
        late = order[-2:]
        for c in order:
            s = s_s[c:c + rc, 0:diag_cols(c)]
            if c not in late:
                fetch(c)
            update(c, s, diag, diag_cols(c), True)
        for c in late:
            fetch(c)
        lf = lam_ref[0]
        lam_full = (jnp.exp(jnp.sum(lf[0:1] * lf[1:2], axis=-1, keepdims=True))
                    - jnp.exp(jnp.sum(lf[2:3] * lf[3:4], axis=-1, keepdims=True))
                    + lambda_init)
        for j in range(nh):
            a0, a1 = j * rows, j * rows + tq
            o = (acc_s[a0:a0 + tq, 0:LANES] / acc_s[a0:a0 + tq, LANES:2 * LANES]
                 - lam_full * (acc_s[a1:a1 + tq, 0:LANES] / acc_s[a1:a1 + tq, LANES:2 * LANES]))
            o_ref[0, :, j * LANES:(j + 1) * LANES] = (
                _rms(o, g_ref[...]) * (1.0 - lambda_init)).astype(BF16)
        reset_max()

    pl.when(qi < nq - 1)(functools.partial(diagonal_stage, False))
    pl.when(qi == nq - 1)(functools.partial(diagonal_stage, True))


def _diff_attn(proj3d, lam, subln_g_row, lambda_init, *, tq=1024, rc=256, nh=2):
    b, s, _ = proj3d.shape
    width = nh * LANES
    groups = DA_HEADS // nh
    qb, kb, vb = _Q_OFF // width, _K_OFF // width, _V_OFF // width

    def next_group(i, h):
        n = jnp.minimum(i * groups + h + 1, b * groups - 1)
        return n // groups, n % groups

    def next_tile0(col0):
        def index_map(i, h, t):
            ni, nh_ = next_group(i, h)
            return ni, 0, col0 + nh_
        return index_map

    return pl.pallas_call(
        functools.partial(_diff_attn_kernel, tq=tq, rc=rc, nq=s // tq, nh=nh,
                          lambda_init=lambda_init),
        grid=(b, groups, s // tq),
        in_specs=[
            pl.BlockSpec((1, 4, DA_QK_DIM), lambda i, h, t: (0, 0, 0)),
            pl.BlockSpec((1, s, width), lambda i, h, t: (i, 0, qb + h)),
            pl.BlockSpec((1, s, width), lambda i, h, t: (i, 0, kb + h)),
            pl.BlockSpec((1, s, width), lambda i, h, t: (i, 0, vb + h)),
            pl.BlockSpec((1, tq, width), next_tile0(qb)),
            pl.BlockSpec((1, tq, width), next_tile0(kb)),
            pl.BlockSpec((1, LANES), lambda i, h, t: (0, 0)),
        ],
        out_specs=pl.BlockSpec((1, tq, width), lambda i, h, t: (i, t, h)),
        out_shape=jax.ShapeDtypeStruct((b, s, MIX_WIDTH), BF16),
        scratch_shapes=[
            pltpu.VMEM((nh * 2 * tq, LANES), BF16),
            pltpu.VMEM((nh, s, 2 * LANES), BF16),
            pltpu.VMEM((nh * 2 * tq, tq), F32),
            pltpu.VMEM((nh * 2 * tq, LANES), F32),
            pltpu.VMEM((nh * 2 * tq, 2 * LANES), F32),
        ],
        compiler_params=pltpu.CompilerParams(
            dimension_semantics=("arbitrary", "arbitrary", "arbitrary"),
            vmem_limit_bytes=VMEM_LIMIT_BYTES),
        name="diff_attn",
    )(lam, proj3d, proj3d, proj3d, proj3d, proj3d, subln_g_row)


def _mem_attention(qm, kv_ref, z_s, sgate_m, rows):
    for hd in range(MEM_HEADS):
        lo = hd * MEM_HEAD_DIM
        k = kv_ref[0, 0, :, lo:lo + MEM_HEAD_DIM]
        v = kv_ref[0, 0, :, MEM_WIDTH + lo:MEM_WIDTH + lo + MEM_HEAD_DIM]
        s = lax.dot_general(qm[:, lo:lo + MEM_HEAD_DIM], k, (((1,), (1,)), ((), ())),
                            preferred_element_type=F32) * (MEM_HEAD_DIM ** -0.5)
        p = jnp.exp(s - jnp.max(s, axis=-1, keepdims=True))
        l = jnp.sum(p, axis=-1, keepdims=True)
        m = jnp.dot(p.astype(BF16), v, preferred_element_type=F32) / l
        z_s[rows, MIX_WIDTH + lo:MIX_WIDTH + lo + MEM_HEAD_DIM] = (
            m * sgate_m[:, lo:lo + MEM_HEAD_DIM].astype(F32)).astype(BF16)


def _tail0_kernel(y_ref, sga_ref, sgb_ref, qm_ref, h_ref, kv_ref, wo_ref, o_ref, z_s, *, sub):
    n_b = MIX_WIDTH - GATE_BLOCK
    for r0 in range(0, z_s.shape[0], sub):
        rows = pl.ds(r0, sub)
        z_s[rows, 0:GATE_BLOCK] = y_ref[rows, 0:GATE_BLOCK] * sga_ref[rows, :]
        z_s[rows, GATE_BLOCK:MIX_WIDTH] = y_ref[rows, GATE_BLOCK:MIX_WIDTH] * sgb_ref[rows, 0:n_b]
        _mem_attention(qm_ref[rows, :], kv_ref, z_s, sgb_ref[rows, n_b:GATE_BLOCK], rows)
        o_ref[rows, :] = h_ref[rows, :] + jnp.dot(z_s[rows, :], wo_ref[0],
                                                  preferred_element_type=F32)


def _tail0(y2d, proj2d, h2d, kv0, wo_bf16, *, seq, tm=1024, sub=512):
    t, d = h2d.shape
    nt = seq // tm
    return pl.pallas_call(
        functools.partial(_tail0_kernel, sub=sub),
        grid=(t // tm,),
        in_specs=[
            pl.BlockSpec((tm, MIX_WIDTH), lambda i: (i, 0)),
            pl.BlockSpec((tm, GATE_BLOCK), lambda i: (i, _GATE_OFF // GATE_BLOCK)),
            pl.BlockSpec((tm, GATE_BLOCK), lambda i: (i, _GATE_OFF // GATE_BLOCK + 1)),
            pl.BlockSpec((tm, MEM_WIDTH), lambda i: (i, _QM_OFF // MEM_WIDTH)),
            pl.BlockSpec((tm, d), lambda i: (i, 0)),
            pl.BlockSpec((1, 1) + kv0.shape[2:], lambda i: (0, i // nt, 0, 0)),
            pl.BlockSpec((1, BRANCH_WIDTH, d), lambda i: (0, 0, 0), pipeline_mode=pl.Buffered(1)),
        ],
        out_specs=pl.BlockSpec((tm, d), lambda i: (i, 0)),
        out_shape=jax.ShapeDtypeStruct((t, d), F32),
        scratch_shapes=[pltpu.VMEM((tm, BRANCH_WIDTH), BF16)],
        compiler_params=pltpu.CompilerParams(
            dimension_semantics=("arbitrary",),
            vmem_limit_bytes=VMEM_LIMIT_BYTES),
        name="tail0",
    )(y2d, proj2d, proj2d, proj2d, h2d, kv0, wo_bf16)


def _layer1_kernel(h_ref, g_ref, win_ref, wg_ref, ps_ref, kv_ref, wo_ref, fg_ref, o_ref,
                   u_s, z_s, *, tm, sub):
    t_idx = pl.program_id(1)

    @pl.when(t_idx == 0)
    def _():
        u_s[0:POOL_HALO, :] = jnp.zeros((POOL_HALO, MIX_WIDTH), F32)

    @pl.when(t_idx > 0)
    def _():
        u_s[0:POOL_HALO, :] = u_s[tm:tm + POOL_HALO, :]

    for r0 in range(0, tm, sub):
        rows = pl.ds(r0, sub)
        h = h_ref[rows, :]
        hn = _rms(h, g_ref[...]).astype(BF16)
        u0 = POOL_HALO + r0
        u_s[u0:u0 + sub, :] = jnp.dot(hn, win_ref[:, 0:MIX_WIDTH], preferred_element_type=F32)
        qm = jnp.dot(hn, win_ref[:, MIX_WIDTH:MIX_WIDTH + MEM_WIDTH],
                     preferred_element_type=F32).astype(BF16)
        gate = jnp.dot(hn, win_ref[:, MIX_WIDTH + MEM_WIDTH:], preferred_element_type=F32)

        pos = t_idx * tm + r0 + lax.broadcasted_iota(jnp.int32, (sub, 1), 0)
        for gi, win in enumerate(POOL_WINDOWS):
            lo = gi * POOL_GROUP_WIDTH
            hi = lo + POOL_GROUP_WIDTH
            w = u_s[u0 - POOL_HALO:u0 + sub, lo:hi]
            ug = w[POOL_HALO:, :]
            step = 1
            while step < win:
                w = w + pltpu.roll(w, step, 0)
                step *= 2
            wsum = w[POOL_HALO:, :]
            inv_cnt = 1.0 / jnp.minimum(pos + 1, win).astype(F32)
            pooled = (wsum * inv_cnt - ug).astype(BF16)
            mixed = jnp.dot(pooled, wg_ref[gi], preferred_element_type=F32) * ps_ref[:, lo:hi]
            z_s[rows, lo:hi] = (mixed * _silu(gate[:, lo:hi])).astype(BF16)

        _mem_attention(qm, kv_ref, z_s, _silu(gate[:, MIX_WIDTH:]), rows)
        h2 = h + jnp.dot(z_s[rows, :], wo_ref[0], preferred_element_type=F32)
        o_ref[rows, :] = _rms(h2, fg_ref[...])


def _layer1(h2d, g_row, win_bf16, wg_bf16, ps_row, kv1, wo_bf16, fg_row, *, seq, tm=1024, sub=512):
    t, d = h2d.shape
    b = t // seq
    nt = seq // tm
    const2 = lambda i, j: (0, 0)
    resident = dict(pipeline_mode=pl.Buffered(1))
    return pl.pallas_call(
        functools.partial(_layer1_kernel, tm=tm, sub=sub),
        grid=(b, nt),
        in_specs=[
            pl.BlockSpec((tm, d), lambda i, j: (i * nt + j, 0)),
            pl.BlockSpec((1, d), const2),
            pl.BlockSpec((d, POOL_IN_WIDTH), const2, **resident),
            pl.BlockSpec(wg_bf16.shape, lambda i, j: (0, 0, 0), **resident),
            pl.BlockSpec((1, MIX_WIDTH), const2),
            pl.BlockSpec((1, 1) + kv1.shape[2:], lambda i, j: (1, i, 0, 0)),
            pl.BlockSpec((1, BRANCH_WIDTH, d), lambda i, j: (1, 0, 0), **resident),
            pl.BlockSpec((1, d), const2),
        ],
        out_specs=pl.BlockSpec((tm, d), lambda i, j: (i * nt + j, 0)),
        out_shape=jax.ShapeDtypeStruct((t, d), F32),
        scratch_shapes=[
            pltpu.VMEM((POOL_HALO + tm, MIX_WIDTH), F32),
            pltpu.VMEM((tm, BRANCH_WIDTH), BF16),
        ],
        compiler_params=pltpu.CompilerParams(
            dimension_semantics=("arbitrary", "arbitrary"),
            vmem_limit_bytes=VMEM_LIMIT_BYTES),
        name="layer1",
    )(h2d, g_row, win_bf16, wg_bf16, ps_row, kv1, wo_bf16, fg_row)


def kernel(x, mem, positions, ln_g, attn_w_in, attn_lambda, attn_subln_g, pool_w_in,
           pool_w_group, pool_scale, mem_norm_g, mem_w_kv, w_out, final_g):
    b, s, d = x.shape
    t = b * s
    lambda_init0 = 0.8 - 0.6 * math.exp(-0.3 * 0)

    inv = ROPE_THETA ** (-jnp.arange(0, DA_QK_DIM, 2, dtype=F32) / DA_QK_DIM)
    inv_row = jnp.tile(inv, LANES // inv.shape[0]).reshape(1, LANES)

    h2d = x.reshape(t, d)
    kv, w0 = _mem_kv(mem, mem_norm_g, mem_w_kv, attn_w_in[0])

    n_grp, gw = pool_w_group.shape[1], pool_w_group.shape[2]
    proj, w1, wg, wo = _inproj0(
        h2d, positions.reshape(-1, LANES // (DA_QK_DIM // 2), LANES), inv_row, ln_g[0].reshape(1, d), w0,
        (pool_w_in[0], pool_w_group[0].reshape(n_grp * gw, gw), w_out.reshape(-1, d)))
    wg = wg.reshape(n_grp, gw, gw)
    wo = wo.reshape(w_out.shape)
    y = _diff_attn(proj.reshape(b, s, ATTN_IN_WIDTH), attn_lambda[0:1],
                   attn_subln_g[0].reshape(1, DA_V_DIM), lambda_init0)
    h1 = _tail0(y.reshape(t, MIX_WIDTH), proj, h2d, kv, wo, seq=s)
    out = _layer1(h1, ln_g[1].reshape(1, d), w1, wg, pool_scale[0].reshape(1, MIX_WIDTH),
                  kv, wo, final_g.reshape(1, d), seq=s)
    return out.reshape(b, s, d)
```

```python
import functools
import math

import jax
import jax.numpy as jnp
from jax import lax
from jax.experimental import pallas as pl
from jax.experimental.pallas import tpu as pltpu

F32 = jnp.float32
BF16 = jnp.bfloat16

D_MODEL = 1024
MEM_HEADS = 4
MEM_HEAD_DIM = 128
MEM_WIDTH = MEM_HEADS * MEM_HEAD_DIM
BRANCH_WIDTH = 2 * D_MODEL
MIX_WIDTH = BRANCH_WIDTH - MEM_WIDTH
DA_QK_DIM = 64
DA_V_DIM = 2 * DA_QK_DIM
DA_HEADS = MIX_WIDTH // DA_V_DIM
ROPE_THETA = 10000.0
POOL_WINDOWS = (2, 4, 8, 16)
POOL_GROUP_WIDTH = MIX_WIDTH // len(POOL_WINDOWS)
POOL_HALO = 16
EPS = 1e-6
ATTN_IN_WIDTH = 3 * MIX_WIDTH + MEM_WIDTH + BRANCH_WIDTH
POOL_IN_WIDTH = MIX_WIDTH + MEM_WIDTH + BRANCH_WIDTH

LANES = 128
VMEM_LIMIT_BYTES = 56 * 1024 * 1024

_Q_OFF = 0
_K_OFF = _Q_OFF + MIX_WIDTH
_V_OFF = _K_OFF + MIX_WIDTH
_QM_OFF = _V_OFF + MIX_WIDTH
_GATE_OFF = _QM_OFF + MEM_WIDTH
GATE_BLOCK = D_MODEL


def _rms(x, g):
    ms = jnp.mean(x * x, axis=-1, keepdims=True)
    return x * lax.rsqrt(ms + EPS) * g


def _silu(x):
    hx = 0.5 * x
    return hx + hx * jnp.tanh(hx)


SIDE_SLOTS = 3


def _mem_kv_kernel(mem_ref, g_ref, w_ref, side_hbm, o_ref, side_o_ref, ring_s, sem, *, steps):
    step = pl.program_id(0) * pl.num_programs(1) + pl.program_id(1)
    side_rows = ring_s.shape[1]

    def block_copy(k):
        slot = k % SIDE_SLOTS
        rows = pl.ds(pl.multiple_of(k * side_rows, side_rows), side_rows)
        return pltpu.make_async_copy(side_hbm.at[rows, :], ring_s.at[slot], sem.at[slot])

    @pl.when(step == 0)
    def _():
        for k in range(min(SIDE_SLOTS - 1, steps)):
            block_copy(k).start()

    @pl.when(step + SIDE_SLOTS - 1 < steps)
    def _():
        block_copy(step + SIDE_SLOTS - 1).start()

    block_copy(step).wait()
    side_o_ref[...] = ring_s[step % SIDE_SLOTS].astype(BF16)
    mem_n = _rms(mem_ref[0], g_ref[...]).astype(BF16)
    o_ref[0, 0] = jnp.dot(mem_n, w_ref[0].astype(BF16), preferred_element_type=F32).astype(BF16)


def _mem_kv(mem, mem_norm_g, w_kv, side_f32):
    depth = w_kv.shape[0]
    b, m, d = mem.shape
    steps = depth * b
    side_rows = side_f32.shape[0] // steps
    assert side_rows * steps == side_f32.shape[0] and side_rows % 16 == 0
    side_spec = pl.BlockSpec((side_rows, side_f32.shape[1]), lambda l, i: (l * b + i, 0))
    return pl.pallas_call(
        functools.partial(_mem_kv_kernel, steps=steps),
        grid=(depth, b),
        in_specs=[
            pl.BlockSpec((1, m, d), lambda l, i: (i, 0, 0)),
            pl.BlockSpec((1, d), lambda l, i: (0, 0)),
            pl.BlockSpec((1, d, 2 * MEM_WIDTH), lambda l, i: (l, 0, 0)),
            pl.BlockSpec(memory_space=pl.ANY),
        ],
        out_specs=[pl.BlockSpec((1, 1, m, 2 * MEM_WIDTH), lambda l, i: (l, i, 0, 0)), side_spec],
        out_shape=[jax.ShapeDtypeStruct((depth, b, m, 2 * MEM_WIDTH), BF16),
                   jax.ShapeDtypeStruct(side_f32.shape, BF16)],
        scratch_shapes=[
            pltpu.VMEM((SIDE_SLOTS, side_rows, side_f32.shape[1]), F32),
            pltpu.SemaphoreType.DMA((SIDE_SLOTS,)),
        ],
        compiler_params=pltpu.CompilerParams(
            dimension_semantics=("arbitrary", "arbitrary"),
            vmem_limit_bytes=VMEM_LIMIT_BYTES),
        name="mem_kv",
    )(mem, mem_norm_g.reshape(1, d), w_kv, side_f32)


def _inproj0_kernel(pos_ref, inv_ref, x_ref, g_ref, w_ref, *refs, tn, n_side):
    side_in, o_ref, side_out = refs[:n_side], refs[n_side], refs[n_side + 1:2 * n_side + 1]
    hn_s, cos_s, sin_s = refs[2 * n_side + 1:]
    for src, dst in zip(side_in, side_out):
        dst[...] = src[...].astype(BF16)

    lane = lax.broadcasted_iota(jnp.int32, (1, LANES), 1)
    first_half = (lane % DA_QK_DIM) < (DA_QK_DIM // 2)
    hn_s[...] = _rms(x_ref[...], g_ref[...]).astype(BF16)
    n_grp = LANES // (DA_QK_DIM // 2)
    grp = lane // (DA_QK_DIM // 2)
    rb = pos_ref.shape[0] // n_grp
    ang = None
    for g in range(n_grp):
        a = pos_ref[g * rb:(g + 1) * rb, :].astype(F32) * inv_ref[...]
        ang = a if ang is None else jnp.where(grp == g, a, ang)
    for packed, dst, signed in ((jnp.cos(ang), cos_s, False), (jnp.sin(ang), sin_s, True)):
        shifted = [packed] + [pltpu.roll(packed, k * (DA_QK_DIM // 2), 1) for k in range(1, n_grp)]
        for g in range(n_grp):
            tbl = shifted[0]
            for k in range(1, n_grp):
                tbl = jnp.where(grp == (g + k) % n_grp, shifted[k], tbl)
            if signed:
                tbl = jnp.where(first_half, -tbl, tbl)
            dst[g * rb:(g + 1) * rb, :] = tbl

    qscale = DA_QK_DIM ** -0.5 * math.log2(math.e)
    for lo in sorted(range(0, w_ref.shape[1], tn), key=lambda c: c < _GATE_OFF):
        acc = jnp.dot(hn_s[...], w_ref[:, lo:lo + tn], preferred_element_type=F32)
        if _Q_OFF <= lo < _V_OFF:
            scale = qscale if lo < _K_OFF else 1.0
            cos = cos_s[...] * scale
            sin = sin_s[...] * scale
            for c in range(0, tn, LANES):
                xc = acc[:, c:c + LANES]
                rot = jnp.where(first_half,
                                pltpu.roll(xc, LANES - DA_QK_DIM // 2, 1),
                                pltpu.roll(xc, DA_QK_DIM // 2, 1))
                o_ref[:, lo + c:lo + c + LANES] = (xc * cos + rot * sin).astype(BF16)
        elif lo >= _GATE_OFF:
            o_ref[:, lo:lo + tn] = _silu(acc).astype(BF16)
        else:
            o_ref[:, lo:lo + tn] = acc.astype(BF16)


def _inproj0(h2d, pos2d, inv_row, g_row, w_bf16, side_f32, *, tm=512, tn=512):
    t, d = h2d.shape
    n = w_bf16.shape[1]
    steps = t // tm
    side_blocks = [(a.shape[0] // steps, a.shape[1]) for a in side_f32]
    assert all(blk[0] * steps == a.shape[0] and blk[0] % 16 == 0
               for blk, a in zip(side_blocks, side_f32))
    side_specs = [pl.BlockSpec(blk, lambda i: (i, 0)) for blk in side_blocks]
    return pl.pallas_call(
        functools.partial(_inproj0_kernel, tn=tn, n_side=len(side_f32)),
        grid=(steps,),
        in_specs=[
            pl.BlockSpec((tm, 1), lambda i: (i, 0)),
            pl.BlockSpec((1, LANES), lambda i: (0, 0)),
            pl.BlockSpec((tm, d), lambda i: (i, 0)),
            pl.BlockSpec((1, d), lambda i: (0, 0)),
            pl.BlockSpec((d, n), lambda i: (0, 0), pipeline_mode=pl.Buffered(1)),
        ] + side_specs,
        out_specs=[pl.BlockSpec((tm, n), lambda i: (i, 0))] + side_specs,
        out_shape=[jax.ShapeDtypeStruct((t, n), BF16)]
        + [jax.ShapeDtypeStruct(a.shape, BF16) for a in side_f32],
        scratch_shapes=[
            pltpu.VMEM((tm, d), BF16),
            pltpu.VMEM((tm, LANES), F32),
            pltpu.VMEM((tm, LANES), F32),
        ],
        compiler_params=pltpu.CompilerParams(
            dimension_semantics=("arbitrary",),
            vmem_limit_bytes=VMEM_LIMIT_BYTES),
        name="inproj0",
    )(pos2d, inv_row, h2d, g_row, w_bf16, *side_f32)


def _diff_attn_kernel(lam_ref, q_ref, k_ref, v_ref, qn_ref, kn_ref, g_ref, o_ref,
                      qs_s, vx_s, s_s, m_s, acc_s, *, tq, rc, nq, nh, lambda_init):
    qi = pl.program_id(2)
    rows = 2 * tq
    chunks = [j * rows + r for r in range(0, rows, rc) for j in range(nh)]

    def head_lanes(c):
        j = c // rows
        return slice(j * LANES, (j + 1) * LANES)

    def stack_q(q):
        lane = lax.broadcasted_iota(jnp.int32, (tq, LANES), 1)
        for j in range(nh):
            qj = q[:, j * LANES:(j + 1) * LANES]
            zero = jnp.zeros_like(qj)
            qs_s[j * rows:j * rows + tq, :] = jnp.where(lane < DA_QK_DIM, qj, zero)
            qs_s[j * rows + tq:(j + 1) * rows, :] = jnp.where(lane >= DA_QK_DIM, qj, zero)

    def reset_max():
        m_s[...] = jnp.full(m_s.shape, -jnp.inf, F32)

    def diag_cols(c):
        return c % tq + rc

    def put_scores(c, k):
        ncols = k.shape[0]
        s_s[c:c + rc, 0:ncols] = lax.dot_general(qs_s[c:c + rc, :], k, (((1,), (1,)), ((), ())),
                                                 preferred_element_type=F32)

    def update(c, s, start, ncols, masked):
        vx = vx_s[c // rows, pl.ds(start, ncols), :]
        cols = [s[:, i:i + LANES] for i in range(0, ncols, LANES)]
        if masked:
            row = lax.broadcasted_iota(jnp.int32, (rc, LANES), 0)
            col = lax.broadcasted_iota(jnp.int32, (rc, LANES), 1)
            for i in range(rc // LANES):
                ci = (ncols - rc) // LANES + i
                cols[ci] = jnp.where(col + i * LANES <= row, cols[ci], -jnp.inf)
        m_prev = m_s[c:c + rc, :]
        m_cur = cols[0]
        for x in cols[1:]:
            m_cur = jnp.maximum(m_cur, x)
        m_new = jnp.maximum(m_prev, jnp.max(m_cur, axis=-1, keepdims=True))
        alpha = jnp.exp2(m_prev - m_new)
        p = jnp.concatenate([jnp.exp2(x - m_new) for x in cols], axis=1).astype(BF16)
        pv = jnp.dot(p, vx, preferred_element_type=F32)
        acc_s[c:c + rc, :] = jnp.concatenate([alpha, alpha], axis=1) * acc_s[c:c + rc, :] + pv
        m_s[c:c + rc, :] = m_new

    @pl.when(qi == 0)
    def _():
        for j in range(nh):
            vx_s[j, :, 0:LANES] = v_ref[0, :, j * LANES:(j + 1) * LANES]

    @pl.when((pl.program_id(0) == 0) & (pl.program_id(1) == 0) & (qi == 0))
    def _():
        vx_s[:, :, LANES:2 * LANES] = jnp.ones((nh, vx_s.shape[1], LANES), BF16)
        reset_max()
        acc_s[...] = jnp.zeros(acc_s.shape, F32)
        stack_q(q_ref[0, 0:tq, :])
        for c in chunks:
            put_scores(c, k_ref[0, 0:diag_cols(c), head_lanes(c)])

    def full_stage(kt, next_is_diag):
        start = pl.multiple_of(kt * tq, tq)
        for c in chunks:
            s = s_s[c:c + rc, :]
            put_scores(c, k_ref[0, pl.ds(start + tq, diag_cols(c) if next_is_diag else tq),
                                head_lanes(c)])
            update(c, s, start, tq, False)

    def body(kt, carry):
        full_stage(kt, False)
        return carry

    lax.fori_loop(0, qi - 1, body, 0)
    pl.when(qi > 0)(lambda: full_stage(qi - 1, True))

    def diagonal_stage(last):
        diag = pl.multiple_of(qi * tq, tq)
        stack_q(qn_ref[0] if last else q_ref[0, pl.ds(diag + tq, tq), :])

        def fetch(c):
            put_scores(c, kn_ref[0, 0:diag_cols(c), head_lanes(c)] if last
                       else k_ref[0, 0:tq, head_lanes(c)])

        order = sorted(chunks, key=diag_cols, reverse=True)
        late = order[-2:]
        for c in order:
            s = s_s[c:c + rc, 0:diag_cols(c)]
            if c not in late:
                fetch(c)
            update(c, s, diag, diag_cols(c), True)
        for c in late:
            fetch(c)
        lf = lam_ref[0]
        lam_full = (jnp.exp(jnp.sum(lf[0:1] * lf[1:2], axis=-1, keepdims=True))
                    - jnp.exp(jnp.sum(lf[2:3] * lf[3:4], axis=-1, keepdims=True))
                    + lambda_init)
        for j in range(nh):
            a0, a1 = j * rows, j * rows + tq
            o = (acc_s[a0:a0 + tq, 0:LANES] / acc_s[a0:a0 + tq, LANES:2 * LANES]
                 - lam_full * (acc_s[a1:a1 + tq, 0:LANES] / acc_s[a1:a1 + tq, LANES:2 * LANES]))
            o_ref[0, :, j * LANES:(j + 1) * LANES] = (
                _rms(o, g_ref[...]) * (1.0 - lambda_init)).astype(BF16)
        reset_max()

    pl.when(qi < nq - 1)(functools.partial(diagonal_stage, False))
    pl.when(qi == nq - 1)(functools.partial(diagonal_stage, True))


def _diff_attn(proj3d, lam, subln_g_row, lambda_init, *, tq=1024, rc=256, nh=2):
    b, s, _ = proj3d.shape
    width = nh * LANES
    groups = DA_HEADS // nh
    qb, kb, vb = _Q_OFF // width, _K_OFF // width, _V_OFF // width

    def next_group(i, h):
        n = jnp.minimum(i * groups + h + 1, b * groups - 1)
        return n // groups, n % groups

    def next_tile0(col0):
        def index_map(i, h, t):
            ni, nh_ = next_group(i, h)
            return ni, 0, col0 + nh_
        return index_map

    return pl.pallas_call(
        functools.partial(_diff_attn_kernel, tq=tq, rc=rc, nq=s // tq, nh=nh,
                          lambda_init=lambda_init),
        grid=(b, groups, s // tq),
        in_specs=[
            pl.BlockSpec((1, 4, DA_QK_DIM), lambda i, h, t: (0, 0, 0)),
            pl.BlockSpec((1, s, width), lambda i, h, t: (i, 0, qb + h)),
            pl.BlockSpec((1, s, width), lambda i, h, t: (i, 0, kb + h)),
            pl.BlockSpec((1, s, width), lambda i, h, t: (i, 0, vb + h)),
            pl.BlockSpec((1, tq, width), next_tile0(qb)),
            pl.BlockSpec((1, tq, width), next_tile0(kb)),
            pl.BlockSpec((1, LANES), lambda i, h, t: (0, 0)),
        ],
        out_specs=pl.BlockSpec((1, tq, width), lambda i, h, t: (i, t, h)),
        out_shape=jax.ShapeDtypeStruct((b, s, MIX_WIDTH), BF16),
        scratch_shapes=[
            pltpu.VMEM((nh * 2 * tq, LANES), BF16),
            pltpu.VMEM((nh, s, 2 * LANES), BF16),
            pltpu.VMEM((nh * 2 * tq, tq), F32),
            pltpu.VMEM((nh * 2 * tq, LANES), F32),
            pltpu.VMEM((nh * 2 * tq, 2 * LANES), F32),
        ],
        compiler_params=pltpu.CompilerParams(
            dimension_semantics=("arbitrary", "arbitrary", "arbitrary"),
            vmem_limit_bytes=VMEM_LIMIT_BYTES),
        name="diff_attn",
    )(lam, proj3d, proj3d, proj3d, proj3d, proj3d, subln_g_row)


def _mem_attention(qm, kv_ref, z_s, sgate_m, rows):
    for hd in range(MEM_HEADS):
        lo = hd * MEM_HEAD_DIM
        k = kv_ref[0, 0, :, lo:lo + MEM_HEAD_DIM]
        v = kv_ref[0, 0, :, MEM_WIDTH + lo:MEM_WIDTH + lo + MEM_HEAD_DIM]
        s = lax.dot_general(qm[:, lo:lo + MEM_HEAD_DIM], k, (((1,), (1,)), ((), ())),
                            preferred_element_type=F32) * (MEM_HEAD_DIM ** -0.5)
        p = jnp.exp(s - jnp.max(s, axis=-1, keepdims=True))
        l = jnp.sum(p, axis=-1, keepdims=True)
        m = jnp.dot(p.astype(BF16), v, preferred_element_type=F32) / l
        z_s[rows, MIX_WIDTH + lo:MIX_WIDTH + lo + MEM_HEAD_DIM] = (
            m * sgate_m[:, lo:lo + MEM_HEAD_DIM].astype(F32)).astype(BF16)


def _tail0_kernel(y_ref, sga_ref, sgb_ref, qm_ref, h_ref, kv_ref, wo_ref, o_ref, z_s, *, sub):
    n_b = MIX_WIDTH - GATE_BLOCK
    for r0 in range(0, z_s.shape[0], sub):
        rows = pl.ds(r0, sub)
        z_s[rows, 0:GATE_BLOCK] = y_ref[rows, 0:GATE_BLOCK] * sga_ref[rows, :]
        z_s[rows, GATE_BLOCK:MIX_WIDTH] = y_ref[rows, GATE_BLOCK:MIX_WIDTH] * sgb_ref[rows, 0:n_b]
        _mem_attention(qm_ref[rows, :], kv_ref, z_s, sgb_ref[rows, n_b:GATE_BLOCK], rows)
        o_ref[rows, :] = h_ref[rows, :] + jnp.dot(z_s[rows, :], wo_ref[0],
                                                  preferred_element_type=F32)


def _tail0(y2d, proj2d, h2d, kv0, wo_bf16, *, seq, tm=1024, sub=512):
    t, d = h2d.shape
    nt = seq // tm
    return pl.pallas_call(
        functools.partial(_tail0_kernel, sub=sub),
        grid=(t // tm,),
        in_specs=[
            pl.BlockSpec((tm, MIX_WIDTH), lambda i: (i, 0)),
            pl.BlockSpec((tm, GATE_BLOCK), lambda i: (i, _GATE_OFF // GATE_BLOCK)),
            pl.BlockSpec((tm, GATE_BLOCK), lambda i: (i, _GATE_OFF // GATE_BLOCK + 1)),
            pl.BlockSpec((tm, MEM_WIDTH), lambda i: (i, _QM_OFF // MEM_WIDTH)),
            pl.BlockSpec((tm, d), lambda i: (i, 0)),
            pl.BlockSpec((1, 1) + kv0.shape[2:], lambda i: (0, i // nt, 0, 0)),
            pl.BlockSpec((1, BRANCH_WIDTH, d), lambda i: (0, 0, 0), pipeline_mode=pl.Buffered(1)),
        ],
        out_specs=pl.BlockSpec((tm, d), lambda i: (i, 0)),
        out_shape=jax.ShapeDtypeStruct((t, d), F32),
        scratch_shapes=[pltpu.VMEM((tm, BRANCH_WIDTH), BF16)],
        compiler_params=pltpu.CompilerParams(
            dimension_semantics=("arbitrary",),
            vmem_limit_bytes=VMEM_LIMIT_BYTES),
        name="tail0",
    )(y2d, proj2d, proj2d, proj2d, h2d, kv0, wo_bf16)


def _layer1_kernel(h_ref, g_ref, win_ref, wg_ref, ps_ref, kv_ref, wo_ref, fg_ref, o_ref,
                   u_s, z_s, *, tm, sub):
    t_idx = pl.program_id(1)

    @pl.when(t_idx == 0)
    def _():
        u_s[0:POOL_HALO, :] = jnp.zeros((POOL_HALO, MIX_WIDTH), F32)

    @pl.when(t_idx > 0)
    def _():
        u_s[0:POOL_HALO, :] = u_s[tm:tm + POOL_HALO, :]

    for r0 in range(0, tm, sub):
        rows = pl.ds(r0, sub)
        h = h_ref[rows, :]
        hn = _rms(h, g_ref[...]).astype(BF16)
        u0 = POOL_HALO + r0
        u_s[u0:u0 + sub, :] = jnp.dot(hn, win_ref[:, 0:MIX_WIDTH], preferred_element_type=F32)
        qm = jnp.dot(hn, win_ref[:, MIX_WIDTH:MIX_WIDTH + MEM_WIDTH],
                     preferred_element_type=F32).astype(BF16)
        gate = jnp.dot(hn, win_ref[:, MIX_WIDTH + MEM_WIDTH:], preferred_element_type=F32)

        pos = t_idx * tm + r0 + lax.broadcasted_iota(jnp.int32, (sub, 1), 0)
        for gi, win in enumerate(POOL_WINDOWS):
            lo = gi * POOL_GROUP_WIDTH
            hi = lo + POOL_GROUP_WIDTH
            w = u_s[u0 - POOL_HALO:u0 + sub, lo:hi]
            ug = w[POOL_HALO:, :]
            step = 1
            while step < win:
                w = w + pltpu.roll(w, step, 0)
                step *= 2
            wsum = w[POOL_HALO:, :]
            inv_cnt = 1.0 / jnp.minimum(pos + 1, win).astype(F32)
            pooled = (wsum * inv_cnt - ug).astype(BF16)
            mixed = jnp.dot(pooled, wg_ref[gi], preferred_element_type=F32) * ps_ref[:, lo:hi]
            z_s[rows, lo:hi] = (mixed * _silu(gate[:, lo:hi])).astype(BF16)

        _mem_attention(qm, kv_ref, z_s, _silu(gate[:, MIX_WIDTH:]), rows)
        h2 = h + jnp.dot(z_s[rows, :], wo_ref[0], preferred_element_type=F32)
        o_ref[rows, :] = _rms(h2, fg_ref[...])


def _layer1(h2d, g_row, win_bf16, wg_bf16, ps_row, kv1, wo_bf16, fg_row, *, seq, tm=1024, sub=512):
    t, d = h2d.shape
    b = t // seq
    nt = seq // tm
    const2 = lambda i, j: (0, 0)
    resident = dict(pipeline_mode=pl.Buffered(1))
    return pl.pallas_call(
        functools.partial(_layer1_kernel, tm=tm, sub=sub),
        grid=(b, nt),
        in_specs=[
            pl.BlockSpec((tm, d), lambda i, j: (i * nt + j, 0)),
            pl.BlockSpec((1, d), const2),
            pl.BlockSpec((d, POOL_IN_WIDTH), const2, **resident),
            pl.BlockSpec(wg_bf16.shape, lambda i, j: (0, 0, 0), **resident),
            pl.BlockSpec((1, MIX_WIDTH), const2),
            pl.BlockSpec((1, 1) + kv1.shape[2:], lambda i, j: (1, i, 0, 0)),
            pl.BlockSpec((1, BRANCH_WIDTH, d), lambda i, j: (1, 0, 0), **resident),
            pl.BlockSpec((1, d), const2),
        ],
        out_specs=pl.BlockSpec((tm, d), lambda i, j: (i * nt + j, 0)),
        out_shape=jax.ShapeDtypeStruct((t, d), F32),
        scratch_shapes=[
            pltpu.VMEM((POOL_HALO + tm, MIX_WIDTH), F32),
            pltpu.VMEM((tm, BRANCH_WIDTH), BF16),
        ],
        compiler_params=pltpu.CompilerParams(
            dimension_semantics=("arbitrary", "arbitrary"),
            vmem_limit_bytes=VMEM_LIMIT_BYTES),
        name="layer1",
    )(h2d, g_row, win_bf16, wg_bf16, ps_row, kv1, wo_bf16, fg_row)


def kernel(x, mem, positions, ln_g, attn_w_in, attn_lambda, attn_subln_g, pool_w_in,
           pool_w_group, pool_scale, mem_norm_g, mem_w_kv, w_out, final_g):
    b, s, d = x.shape
    t = b * s
    lambda_init0 = 0.8 - 0.6 * math.exp(-0.3 * 0)

    inv = ROPE_THETA ** (-jnp.arange(0, DA_QK_DIM, 2, dtype=F32) / DA_QK_DIM)
    inv_row = jnp.tile(inv, LANES // inv.shape[0]).reshape(1, LANES)

    h2d = x.reshape(t, d)
    kv, w0 = _mem_kv(mem, mem_norm_g, mem_w_kv, attn_w_in[0])

    n_grp, gw = pool_w_group.shape[1], pool_w_group.shape[2]
    proj, w1, wg, wo = _inproj0(
        h2d, positions.reshape(t, 1), inv_row, ln_g[0].reshape(1, d), w0,
        (pool_w_in[0], pool_w_group[0].reshape(n_grp * gw, gw), w_out.reshape(-1, d)))
    wg = wg.reshape(n_grp, gw, gw)
    wo = wo.reshape(w_out.shape)
    y = _diff_attn(proj.reshape(b, s, ATTN_IN_WIDTH), attn_lambda[0:1],
                   attn_subln_g[0].reshape(1, DA_V_DIM), lambda_init0)
    h1 = _tail0(y.reshape(t, MIX_WIDTH), proj, h2d, kv, wo, seq=s)
    out = _layer1(h1, ln_g[1].reshape(1, d), w1, wg, pool_scale[0].reshape(1, MIX_WIDTH),
                  kv, wo, final_g.reshape(1, d), seq=s)
    return out.reshape(b, s, d)
```

```python
import functools
import math

import jax
import jax.numpy as jnp
from jax import lax
from jax.experimental import pallas as pl
from jax.experimental.pallas import tpu as pltpu

F32 = jnp.float32
BF16 = jnp.bfloat16

D_MODEL = 1024
MEM_HEADS = 4
MEM_HEAD_DIM = 128
MEM_WIDTH = MEM_HEADS * MEM_HEAD_DIM
BRANCH_WIDTH = 2 * D_MODEL
MIX_WIDTH = BRANCH_WIDTH - MEM_WIDTH
DA_QK_DIM = 64
DA_V_DIM = 2 * DA_QK_DIM
DA_HEADS = MIX_WIDTH // DA_V_DIM
ROPE_THETA = 10000.0
POOL_WINDOWS = (2, 4, 8, 16)
POOL_GROUP_WIDTH = MIX_WIDTH // len(POOL_WINDOWS)
POOL_HALO = 16
EPS = 1e-6
ATTN_IN_WIDTH = 3 * MIX_WIDTH + MEM_WIDTH + BRANCH_WIDTH
POOL_IN_WIDTH = MIX_WIDTH + MEM_WIDTH + BRANCH_WIDTH

LANES = 128
VMEM_LIMIT_BYTES = 56 * 1024 * 1024

_Q_OFF = 0
_K_OFF = _Q_OFF + MIX_WIDTH
_V_OFF = _K_OFF + MIX_WIDTH
_QM_OFF = _V_OFF + MIX_WIDTH
_GATE_OFF = _QM_OFF + MEM_WIDTH
GATE_BLOCK = D_MODEL


def _rms(x, g):
    ms = jnp.mean(x * x, axis=-1, keepdims=True)
    return x * lax.rsqrt(ms + EPS) * g


def _silu(x):
    hx = 0.5 * x
    return hx + hx * jnp.tanh(hx)


def _mem_kv_kernel(mem_ref, g_ref, w_ref, side_ref, o_ref, side_o_ref):
    side_o_ref[...] = side_ref[...].astype(BF16)
    mem_n = _rms(mem_ref[0], g_ref[...]).astype(BF16)
    o_ref[0, 0] = jnp.dot(mem_n, w_ref[0].astype(BF16), preferred_element_type=F32).astype(BF16)


def _mem_kv(mem, mem_norm_g, w_kv, side_f32):
    depth = w_kv.shape[0]
    b, m, d = mem.shape
    steps = depth * b
    side_rows = side_f32.shape[0] // steps
    assert side_rows * steps == side_f32.shape[0] and side_rows % 16 == 0
    side_spec = pl.BlockSpec((side_rows, side_f32.shape[1]), lambda l, i: (l * b + i, 0))
    return pl.pallas_call(
        _mem_kv_kernel,
        grid=(depth, b),
        in_specs=[
            pl.BlockSpec((1, m, d), lambda l, i: (i, 0, 0)),
            pl.BlockSpec((1, d), lambda l, i: (0, 0)),
            pl.BlockSpec((1, d, 2 * MEM_WIDTH), lambda l, i: (l, 0, 0)),
            side_spec,
        ],
        out_specs=[pl.BlockSpec((1, 1, m, 2 * MEM_WIDTH), lambda l, i: (l, i, 0, 0)), side_spec],
        out_shape=[jax.ShapeDtypeStruct((depth, b, m, 2 * MEM_WIDTH), BF16),
                   jax.ShapeDtypeStruct(side_f32.shape, BF16)],
        compiler_params=pltpu.CompilerParams(
            dimension_semantics=("arbitrary", "arbitrary"),
            vmem_limit_bytes=VMEM_LIMIT_BYTES),
        name="mem_kv",
    )(mem, mem_norm_g.reshape(1, d), w_kv, side_f32)


def _inproj0_kernel(pos_ref, inv_ref, x_ref, g_ref, w_ref, *refs, tn, n_side):
    side_in, o_ref, side_out = refs[:n_side], refs[n_side], refs[n_side + 1:2 * n_side + 1]
    hn_s, cos_s, sin_s = refs[2 * n_side + 1:]
    for src, dst in zip(side_in, side_out):
        dst[...] = src[...].astype(BF16)

    lane = lax.broadcasted_iota(jnp.int32, (1, LANES), 1)
    first_half = (lane % DA_QK_DIM) < (DA_QK_DIM // 2)
    hn_s[...] = _rms(x_ref[...], g_ref[...]).astype(BF16)
    n_grp = LANES // (DA_QK_DIM // 2)
    grp = lane // (DA_QK_DIM // 2)
    rb = pos_ref.shape[0] // n_grp
    ang = None
    for g in range(n_grp):
        a = pos_ref[g * rb:(g + 1) * rb, :].astype(F32) * inv_ref[...]
        ang = a if ang is None else jnp.where(grp == g, a, ang)
    for packed, dst, signed in ((jnp.cos(ang), cos_s, False), (jnp.sin(ang), sin_s, True)):
        shifted = [packed] + [pltpu.roll(packed, k * (DA_QK_DIM // 2), 1) for k in range(1, n_grp)]
        for g in range(n_grp):
            tbl = shifted[0]
            for k in range(1, n_grp):
                tbl = jnp.where(grp == (g + k) % n_grp, shifted[k], tbl)
            if signed:
                tbl = jnp.where(first_half, -tbl, tbl)
            dst[g * rb:(g + 1) * rb, :] = tbl

    qscale = DA_QK_DIM ** -0.5 * math.log2(math.e)
    for lo in sorted(range(0, w_ref.shape[1], tn), key=lambda c: c < _GATE_OFF):
        acc = jnp.dot(hn_s[...], w_ref[:, lo:lo + tn], preferred_element_type=F32)
        if _Q_OFF <= lo < _V_OFF:
            scale = qscale if lo < _K_OFF else 1.0
            cos = cos_s[...] * scale
            sin = sin_s[...] * scale
            for c in range(0, tn, LANES):
                xc = acc[:, c:c + LANES]
                rot = jnp.where(first_half,
                                pltpu.roll(xc, LANES - DA_QK_DIM // 2, 1),
                                pltpu.roll(xc, DA_QK_DIM // 2, 1))
                o_ref[:, lo + c:lo + c + LANES] = (xc * cos + rot * sin).astype(BF16)
        elif lo >= _GATE_OFF:
            o_ref[:, lo:lo + tn] = _silu(acc).astype(BF16)
        else:
            o_ref[:, lo:lo + tn] = acc.astype(BF16)


def _inproj0(h2d, pos2d, inv_row, g_row, w_bf16, side_f32, *, tm=512, tn=512):
    t, d = h2d.shape
    n = w_bf16.shape[1]
    steps = t // tm
    side_blocks = [(a.shape[0] // steps, a.shape[1]) for a in side_f32]
    assert all(blk[0] * steps == a.shape[0] and blk[0] % 16 == 0
               for blk, a in zip(side_blocks, side_f32))
    side_specs = [pl.BlockSpec(blk, lambda i: (i, 0)) for blk in side_blocks]
    return pl.pallas_call(
        functools.partial(_inproj0_kernel, tn=tn, n_side=len(side_f32)),
        grid=(steps,),
        in_specs=[
            pl.BlockSpec((tm, 1), lambda i: (i, 0)),
            pl.BlockSpec((1, LANES), lambda i: (0, 0)),
            pl.BlockSpec((tm, d), lambda i: (i, 0)),
            pl.BlockSpec((1, d), lambda i: (0, 0)),
            pl.BlockSpec((d, n), lambda i: (0, 0), pipeline_mode=pl.Buffered(1)),
        ] + side_specs,
        out_specs=[pl.BlockSpec((tm, n), lambda i: (i, 0))] + side_specs,
        out_shape=[jax.ShapeDtypeStruct((t, n), BF16)]
        + [jax.ShapeDtypeStruct(a.shape, BF16) for a in side_f32],
        scratch_shapes=[
            pltpu.VMEM((tm, d), BF16),
            pltpu.VMEM((tm, LANES), F32),
            pltpu.VMEM((tm, LANES), F32),
        ],
        compiler_params=pltpu.CompilerParams(
            dimension_semantics=("arbitrary",),
            vmem_limit_bytes=VMEM_LIMIT_BYTES),
        name="inproj0",
    )(pos2d, inv_row, h2d, g_row, w_bf16, *side_f32)


def _diff_attn_kernel(lam_ref, q_ref, k_ref, v_ref, qn_ref, kn_ref, g_ref, o_ref,
                      qs_s, vx_s, s_s, m_s, acc_s, *, tq, rc, nq, nh, lambda_init):
    qi = pl.program_id(2)
    rows = 2 * tq
    chunks = [j * rows + r for r in range(0, rows, rc) for j in range(nh)]

    def head_lanes(c):
        j = c // rows
        return slice(j * LANES, (j + 1) * LANES)

    def stack_q(q):
        lane = lax.broadcasted_iota(jnp.int32, (tq, LANES), 1)
        for j in range(nh):
            qj = q[:, j * LANES:(j + 1) * LANES]
            zero = jnp.zeros_like(qj)
            qs_s[j * rows:j * rows + tq, :] = jnp.where(lane < DA_QK_DIM, qj, zero)
            qs_s[j * rows + tq:(j + 1) * rows, :] = jnp.where(lane >= DA_QK_DIM, qj, zero)

    def reset_max():
        m_s[...] = jnp.full(m_s.shape, -jnp.inf, F32)

    def diag_cols(c):
        return c % tq + rc

    def put_scores(c, k):
        ncols = k.shape[0]
        s_s[c:c + rc, 0:ncols] = lax.dot_general(qs_s[c:c + rc, :], k, (((1,), (1,)), ((), ())),
                                                 preferred_element_type=F32)

    def update(c, s, start, ncols, masked):
        vx = vx_s[c // rows, pl.ds(start, ncols), :]
        cols = [s[:, i:i + LANES] for i in range(0, ncols, LANES)]
        if masked:
            row = lax.broadcasted_iota(jnp.int32, (rc, LANES), 0)
            col = lax.broadcasted_iota(jnp.int32, (rc, LANES), 1)
            for i in range(rc // LANES):
                ci = (ncols - rc) // LANES + i
                cols[ci] = jnp.where(col + i * LANES <= row, cols[ci], -jnp.inf)
        m_prev = m_s[c:c + rc, :]
        m_cur = cols[0]
        for x in cols[1:]:
            m_cur = jnp.maximum(m_cur, x)
        m_new = jnp.maximum(m_prev, jnp.max(m_cur, axis=-1, keepdims=True))
        alpha = jnp.exp2(m_prev - m_new)
        p = jnp.concatenate([jnp.exp2(x - m_new) for x in cols], axis=1).astype(BF16)
        pv = jnp.dot(p, vx, preferred_element_type=F32)
        acc_s[c:c + rc, :] = jnp.concatenate([alpha, alpha], axis=1) * acc_s[c:c + rc, :] + pv
        m_s[c:c + rc, :] = m_new

    @pl.when(qi == 0)
    def _():
        for j in range(nh):
            vx_s[j, :, 0:LANES] = v_ref[0, :, j * LANES:(j + 1) * LANES]

    @pl.when((pl.program_id(0) == 0) & (pl.program_id(1) == 0) & (qi == 0))
    def _():
        vx_s[:, :, LANES:2 * LANES] = jnp.ones((nh, vx_s.shape[1], LANES), BF16)
        reset_max()
        acc_s[...] = jnp.zeros(acc_s.shape, F32)
        stack_q(q_ref[0, 0:tq, :])
        for c in chunks:
            put_scores(c, k_ref[0, 0:diag_cols(c), head_lanes(c)])

    def full_stage(kt, next_is_diag):
        start = pl.multiple_of(kt * tq, tq)
        for c in (sorted(chunks) if next_is_diag else chunks):
            s = s_s[c:c + rc, :]
            put_scores(c, k_ref[0, pl.ds(start + tq, diag_cols(c) if next_is_diag else tq),
                                head_lanes(c)])
            update(c, s, start, tq, False)

    def body(kt, carry):
        full_stage(kt, False)
        return carry

    lax.fori_loop(0, qi - 1, body, 0)
    pl.when(qi > 0)(lambda: full_stage(qi - 1, True))

    def diagonal_stage(last):
        diag = pl.multiple_of(qi * tq, tq)
        stack_q(qn_ref[0] if last else q_ref[0, pl.ds(diag + tq, tq), :])

        def fetch(c):
            put_scores(c, kn_ref[0, 0:diag_cols(c), head_lanes(c)] if last
                       else k_ref[0, 0:tq, head_lanes(c)])

        order = sorted(chunks, key=diag_cols, reverse=True)
        late = order[-2:]
        for c in order:
            s = s_s[c:c + rc, 0:diag_cols(c)]
            if c not in late:
                fetch(c)
            update(c, s, diag, diag_cols(c), True)
        for c in late:
            fetch(c)
        lf = lam_ref[0]
        lam_full = (jnp.exp(jnp.sum(lf[0:1] * lf[1:2], axis=-1, keepdims=True))
                    - jnp.exp(jnp.sum(lf[2:3] * lf[3:4], axis=-1, keepdims=True))
                    + lambda_init)
        for j in range(nh):
            a0, a1 = j * rows, j * rows + tq
            o = (acc_s[a0:a0 + tq, 0:LANES] / acc_s[a0:a0 + tq, LANES:2 * LANES]
                 - lam_full * (acc_s[a1:a1 + tq, 0:LANES] / acc_s[a1:a1 + tq, LANES:2 * LANES]))
            o_ref[0, :, j * LANES:(j + 1) * LANES] = (
                _rms(o, g_ref[...]) * (1.0 - lambda_init)).astype(BF16)
        reset_max()

    pl.when(qi < nq - 1)(functools.partial(diagonal_stage, False))
    pl.when(qi == nq - 1)(functools.partial(diagonal_stage, True))


def _diff_attn(proj3d, lam, subln_g_row, lambda_init, *, tq=1024, rc=256, nh=2):
    b, s, _ = proj3d.shape
    width = nh * LANES
    groups = DA_HEADS // nh
    qb, kb, vb = _Q_OFF // width, _K_OFF // width, _V_OFF // width

    def next_group(i, h):
        n = jnp.minimum(i * groups + h + 1, b * groups - 1)
        return n // groups, n % groups

    def next_tile0(col0):
        def index_map(i, h, t):
            ni, nh_ = next_group(i, h)
            return ni, 0, col0 + nh_
        return index_map

    return pl.pallas_call(
        functools.partial(_diff_attn_kernel, tq=tq, rc=rc, nq=s // tq, nh=nh,
                          lambda_init=lambda_init),
        grid=(b, groups, s // tq),
        in_specs=[
            pl.BlockSpec((1, 4, DA_QK_DIM), lambda i, h, t: (0, 0, 0)),
            pl.BlockSpec((1, s, width), lambda i, h, t: (i, 0, qb + h)),
            pl.BlockSpec((1, s, width), lambda i, h, t: (i, 0, kb + h)),
            pl.BlockSpec((1, s, width), lambda i, h, t: (i, 0, vb + h)),
            pl.BlockSpec((1, tq, width), next_tile0(qb)),
            pl.BlockSpec((1, tq, width), next_tile0(kb)),
            pl.BlockSpec((1, LANES), lambda i, h, t: (0, 0)),
        ],
        out_specs=pl.BlockSpec((1, tq, width), lambda i, h, t: (i, t, h)),
        out_shape=jax.ShapeDtypeStruct((b, s, MIX_WIDTH), BF16),
        scratch_shapes=[
            pltpu.VMEM((nh * 2 * tq, LANES), BF16),
            pltpu.VMEM((nh, s, 2 * LANES), BF16),
            pltpu.VMEM((nh * 2 * tq, tq), F32),
            pltpu.VMEM((nh * 2 * tq, LANES), F32),
            pltpu.VMEM((nh * 2 * tq, 2 * LANES), F32),
        ],
        compiler_params=pltpu.CompilerParams(
            dimension_semantics=("arbitrary", "arbitrary", "arbitrary"),
            vmem_limit_bytes=VMEM_LIMIT_BYTES),
        name="diff_attn",
    )(lam, proj3d, proj3d, proj3d, proj3d, proj3d, subln_g_row)


def _mem_attention(qm, kv_ref, z_s, sgate_m, rows):
    for hd in range(MEM_HEADS):
        lo = hd * MEM_HEAD_DIM
        k = kv_ref[0, 0, :, lo:lo + MEM_HEAD_DIM]
        v = kv_ref[0, 0, :, MEM_WIDTH + lo:MEM_WIDTH + lo + MEM_HEAD_DIM]
        s = lax.dot_general(qm[:, lo:lo + MEM_HEAD_DIM], k, (((1,), (1,)), ((), ())),
                            preferred_element_type=F32) * (MEM_HEAD_DIM ** -0.5)
        p = jnp.exp(s - jnp.max(s, axis=-1, keepdims=True))
        l = jnp.sum(p, axis=-1, keepdims=True)
        m = jnp.dot(p.astype(BF16), v, preferred_element_type=F32) / l
        z_s[rows, MIX_WIDTH + lo:MIX_WIDTH + lo + MEM_HEAD_DIM] = (
            m * sgate_m[:, lo:lo + MEM_HEAD_DIM].astype(F32)).astype(BF16)


def _tail0_kernel(y_ref, sga_ref, sgb_ref, qm_ref, h_ref, kv_ref, wo_ref, o_ref, z_s, *, sub):
    n_b = MIX_WIDTH - GATE_BLOCK
    for r0 in range(0, z_s.shape[0], sub):
        rows = pl.ds(r0, sub)
        z_s[rows, 0:GATE_BLOCK] = y_ref[rows, 0:GATE_BLOCK] * sga_ref[rows, :]
        z_s[rows, GATE_BLOCK:MIX_WIDTH] = y_ref[rows, GATE_BLOCK:MIX_WIDTH] * sgb_ref[rows, 0:n_b]
        _mem_attention(qm_ref[rows, :], kv_ref, z_s, sgb_ref[rows, n_b:GATE_BLOCK], rows)
        o_ref[rows, :] = h_ref[rows, :] + jnp.dot(z_s[rows, :], wo_ref[0],
                                                  preferred_element_type=F32)


def _tail0(y2d, proj2d, h2d, kv0, wo_bf16, *, seq, tm=1024, sub=512):
    t, d = h2d.shape
    nt = seq // tm
    return pl.pallas_call(
        functools.partial(_tail0_kernel, sub=sub),
        grid=(t // tm,),
        in_specs=[
            pl.BlockSpec((tm, MIX_WIDTH), lambda i: (i, 0)),
            pl.BlockSpec((tm, GATE_BLOCK), lambda i: (i, _GATE_OFF // GATE_BLOCK)),
            pl.BlockSpec((tm, GATE_BLOCK), lambda i: (i, _GATE_OFF // GATE_BLOCK + 1)),
            pl.BlockSpec((tm, MEM_WIDTH), lambda i: (i, _QM_OFF // MEM_WIDTH)),
            pl.BlockSpec((tm, d), lambda i: (i, 0)),
            pl.BlockSpec((1, 1) + kv0.shape[2:], lambda i: (0, i // nt, 0, 0)),
            pl.BlockSpec((1, BRANCH_WIDTH, d), lambda i: (0, 0, 0), pipeline_mode=pl.Buffered(1)),
        ],
        out_specs=pl.BlockSpec((tm, d), lambda i: (i, 0)),
        out_shape=jax.ShapeDtypeStruct((t, d), F32),
        scratch_shapes=[pltpu.VMEM((tm, BRANCH_WIDTH), BF16)],
        compiler_params=pltpu.CompilerParams(
            dimension_semantics=("arbitrary",),
            vmem_limit_bytes=VMEM_LIMIT_BYTES),
        name="tail0",
    )(y2d, proj2d, proj2d, proj2d, h2d, kv0, wo_bf16)


def _layer1_kernel(h_ref, g_ref, win_ref, wg_ref, ps_ref, kv_ref, wo_ref, fg_ref, o_ref,
                   u_s, z_s, *, tm, sub):
    t_idx = pl.program_id(1)

    @pl.when(t_idx == 0)
    def _():
        u_s[0:POOL_HALO, :] = jnp.zeros((POOL_HALO, MIX_WIDTH), F32)

    @pl.when(t_idx > 0)
    def _():
        u_s[0:POOL_HALO, :] = u_s[tm:tm + POOL_HALO, :]

    for r0 in range(0, tm, sub):
        rows = pl.ds(r0, sub)
        h = h_ref[rows, :]
        hn = _rms(h, g_ref[...]).astype(BF16)
        u0 = POOL_HALO + r0
        u_s[u0:u0 + sub, :] = jnp.dot(hn, win_ref[:, 0:MIX_WIDTH], preferred_element_type=F32)
        qm = jnp.dot(hn, win_ref[:, MIX_WIDTH:MIX_WIDTH + MEM_WIDTH],
                     preferred_element_type=F32).astype(BF16)
        gate = jnp.dot(hn, win_ref[:, MIX_WIDTH + MEM_WIDTH:], preferred_element_type=F32)

        pos = t_idx * tm + r0 + lax.broadcasted_iota(jnp.int32, (sub, 1), 0)
        for gi, win in enumerate(POOL_WINDOWS):
            lo = gi * POOL_GROUP_WIDTH
            hi = lo + POOL_GROUP_WIDTH
            w = u_s[u0 - POOL_HALO:u0 + sub, lo:hi]
            ug = w[POOL_HALO:, :]
            step = 1
            while step < win:
                w = w + pltpu.roll(w, step, 0)
                step *= 2
            wsum = w[POOL_HALO:, :]
            inv_cnt = 1.0 / jnp.minimum(pos + 1, win).astype(F32)
            pooled = (wsum * inv_cnt - ug).astype(BF16)
            mixed = jnp.dot(pooled, wg_ref[gi], preferred_element_type=F32) * ps_ref[:, lo:hi]
            z_s[rows, lo:hi] = (mixed * _silu(gate[:, lo:hi])).astype(BF16)

        _mem_attention(qm, kv_ref, z_s, _silu(gate[:, MIX_WIDTH:]), rows)
        h2 = h + jnp.dot(z_s[rows, :], wo_ref[0], preferred_element_type=F32)
        o_ref[rows, :] = _rms(h2, fg_ref[...])


def _layer1(h2d, g_row, win_bf16, wg_bf16, ps_row, kv1, wo_bf16, fg_row, *, seq, tm=1024, sub=512):
    t, d = h2d.shape
    b = t // seq
    nt = seq // tm
    const2 = lambda i, j: (0, 0)
    resident = dict(pipeline_mode=pl.Buffered(1))
    return pl.pallas_call(
        functools.partial(_layer1_kernel, tm=tm, sub=sub),
        grid=(b, nt),
        in_specs=[
            pl.BlockSpec((tm, d), lambda i, j: (i * nt + j, 0)),
            pl.BlockSpec((1, d), const2),
            pl.BlockSpec((d, POOL_IN_WIDTH), const2, **resident),
            pl.BlockSpec(wg_bf16.shape, lambda i, j: (0, 0, 0), **resident),
            pl.BlockSpec((1, MIX_WIDTH), const2),
            pl.BlockSpec((1, 1) + kv1.shape[2:], lambda i, j: (1, i, 0, 0)),
            pl.BlockSpec((1, BRANCH_WIDTH, d), lambda i, j: (1, 0, 0), **resident),
            pl.BlockSpec((1, d), const2),
        ],
        out_specs=pl.BlockSpec((tm, d), lambda i, j: (i * nt + j, 0)),
        out_shape=jax.ShapeDtypeStruct((t, d), F32),
        scratch_shapes=[
            pltpu.VMEM((POOL_HALO + tm, MIX_WIDTH), F32),
            pltpu.VMEM((tm, BRANCH_WIDTH), BF16),
        ],
        compiler_params=pltpu.CompilerParams(
            dimension_semantics=("arbitrary", "arbitrary"),
            vmem_limit_bytes=VMEM_LIMIT_BYTES),
        name="layer1",
    )(h2d, g_row, win_bf16, wg_bf16, ps_row, kv1, wo_bf16, fg_row)


def kernel(x, mem, positions, ln_g, attn_w_in, attn_lambda, attn_subln_g, pool_w_in,
           pool_w_group, pool_scale, mem_norm_g, mem_w_kv, w_out, final_g):
    b, s, d = x.shape
    t = b * s
    lambda_init0 = 0.8 - 0.6 * math.exp(-0.3 * 0)

    inv = ROPE_THETA ** (-jnp.arange(0, DA_QK_DIM, 2, dtype=F32) / DA_QK_DIM)
    inv_row = jnp.tile(inv, LANES // inv.shape[0]).reshape(1, LANES)

    h2d = x.reshape(t, d)
    kv, w0 = _mem_kv(mem, mem_norm_g, mem_w_kv, attn_w_in[0])

    n_grp, gw = pool_w_group.shape[1], pool_w_group.shape[2]
    proj, w1, wg, wo = _inproj0(
        h2d, positions.reshape(t, 1), inv_row, ln_g[0].reshape(1, d), w0,
        (pool_w_in[0], pool_w_group[0].reshape(n_grp * gw, gw), w_out.reshape(-1, d)))
    wg = wg.reshape(n_grp, gw, gw)
    wo = wo.reshape(w_out.shape)
    y = _diff_attn(proj.reshape(b, s, ATTN_IN_WIDTH), attn_lambda[0:1],
                   attn_subln_g[0].reshape(1, DA_V_DIM), lambda_init0)
    h1 = _tail0(y.reshape(t, MIX_WIDTH), proj, h2d, kv, wo, seq=s)
    out = _layer1(h1, ln_g[1].reshape(1, d), w1, wg, pool_scale[0].reshape(1, MIX_WIDTH),
                  kv, wo, final_g.reshape(1, d), seq=s)
    return out.reshape(b, s, d)
```
